```python
import math
import jax, jax.numpy as jnp
from jax import lax
import numpy as np

D_MODEL = 1024
BATCH = 8
SEQ = 2048
DEPTH = 1

NSA_HEADS = 8
NSA_GROUPS = 2
NSA_HPG = NSA_HEADS // NSA_GROUPS
HEAD_DIM = 64
NSA_DIM = NSA_HEADS * HEAD_DIM
NSA_KV_DIM = NSA_GROUPS * HEAD_DIM
CMP_BLOCK = 32
CMP_STRIDE = 16
CMP_HIDDEN = 256
SEL_BLOCK = 64
SEL_TOPN = 16
WINDOW = 512
Q_BLOCK = 128
SEL_Q_BLOCK = 32
FORCE_BONUS = 1.0e4
ATTN_SCALE = HEAD_DIM ** -0.5
RWKV_HEADS = 8
RWKV_HEAD_DIM = 64
RWKV_DIM = RWKV_HEADS * RWKV_HEAD_DIM
LORA_W = 64
LORA_A = 64
LORA_G = 128
RWKV_GN_EPS = 64e-5
N_BUCKETS = 32
MAX_DISTANCE = 128
PEER_HEADS = 8
N_KEYS = 128
N_EXPERTS = N_KEYS * N_KEYS
PEER_KEY_DIM = 128
PEER_HALF = PEER_KEY_DIM // 2
PEER_TOPK = 16
PEER_TOKEN_BLOCK = 128
ALPHA = (2.0 * DEPTH) ** 0.25
BETA = (8.0 * DEPTH) ** -0.25
LN_EPS = 1e-5
NEG = -1e30
RWKV_COLS = 3 * RWKV_DIM + LORA_W + LORA_A + LORA_G
IN_SIZES = (NSA_DIM,) + (NSA_KV_DIM,) * 6 + (3 * NSA_HEADS, RWKV_COLS, 2 * D_MODEL)
D_IN = int(sum(IN_SIZES))
IN_SPLITS = tuple(int(c) for c in np.cumsum(IN_SIZES)[:-1])
RWKV_SPLITS = tuple(int(c) for c in np.cumsum((RWKV_DIM, RWKV_DIM, RWKV_DIM, LORA_W, LORA_A, LORA_G))[:-1])

kernel_name = "nsa_rwkv7_peer_hybrid_deepnorm"


def _layer_norm(x, g, b):
    xf = x.astype(jnp.float32)
    mu = xf.mean(-1, keepdims=True)
    var = jnp.square(xf - mu).mean(-1, keepdims=True)
    return ((xf - mu) * lax.rsqrt(var + LN_EPS) * g + b).astype(x.dtype)


def _t5_bucket(dist):
    max_exact = N_BUCKETS // 2
    d = jnp.maximum(dist, 0)
    large = max_exact + (jnp.log(jnp.maximum(d, 1).astype(jnp.float32) / max_exact)
                         / math.log(MAX_DISTANCE / max_exact) * (N_BUCKETS - max_exact)).astype(jnp.int32)
    large = jnp.minimum(large, N_BUCKETS - 1)
    return jnp.where(d < max_exact, d, large)


def _masked_softmax(logits, mask):
    p = jax.nn.softmax(jnp.where(mask, logits.astype(jnp.float32), NEG), axis=-1)
    return jnp.where(mask, p, 0.0)


def _compress(kv, pos, w1, b1, w2, b2):
    B, S, G, D = kv.shape
    c = kv.reshape(B, S // CMP_STRIDE, CMP_STRIDE, G, D)
    blocks = jnp.concatenate([c[:, :-1], c[:, 1:]], axis=2) + pos[None, None, :, None, :]
    nc = blocks.shape[1]
    flat = blocks.transpose(0, 1, 3, 2, 4).reshape(B, nc, G, CMP_BLOCK * D)
    return jax.nn.gelu(flat @ w1 + b1) @ w2 + b2


def _nsa_compressed(qg, kc, vc, rel_bias):
    S = qg.shape[1]
    nc = kc.shape[1]
    t = jnp.arange(S)
    blk_end = jnp.arange(nc) * CMP_STRIDE + CMP_BLOCK - 1
    dist = t[:, None] - blk_end[None, :]
    mask = dist >= 0
    bias = rel_bias[_t5_bucket(dist)].reshape(S, nc, NSA_GROUPS, NSA_HPG).transpose(2, 3, 0, 1)
    logits = jnp.einsum('bsghd,bcgd->bghsc', qg, kc).astype(jnp.float32) * ATTN_SCALE + bias
    p = _masked_softmax(logits, mask)
    out = jnp.einsum('bghsc,bcgd->bsghd', p.astype(vc.dtype), vc)
    return out, p


def _selection_indices(p_cmp, S):
    nc = p_cmp.shape[-1]
    nsb = S // SEL_BLOCK
    c0 = jnp.arange(nc) * CMP_STRIDE
    s0 = jnp.arange(nsb) * SEL_BLOCK
    overlap = jnp.clip(jnp.minimum(c0[:, None] + CMP_BLOCK, s0[None, :] + SEL_BLOCK)
                       - jnp.maximum(c0[:, None], s0[None, :]), 0, None).astype(jnp.float32) / CMP_BLOCK
    imp = jnp.einsum('bghsc,cj->bgsj', p_cmp, overlap)
    t = jnp.arange(S)
    cur = t // SEL_BLOCK
    j = jnp.arange(nsb)
    forced = (j[None, :] == 0) | (j[None, :] == cur[:, None]) | (j[None, :] == cur[:, None] - 1)
    future = j[None, :] > cur[:, None]
    score = jnp.where(future, NEG, imp + jnp.where(forced, FORCE_BONUS, 0.0))
    n_sel = min(SEL_TOPN, nsb)
    _, idx = lax.top_k(score, n_sel)
    ok = idx <= cur[None, None, :, None]
    return idx, ok


def _nsa_selected(qg, k, v, idx, ok, rel_bias):
    B, S, G, HPG, D = qg.shape
    nsb = S // SEL_BLOCK
    n_sel = idx.shape[-1]
    kb = k.reshape(B, nsb, SEL_BLOCK, G, D).transpose(0, 3, 1, 2, 4)
    vb = v.reshape(B, nsb, SEL_BLOCK, G, D).transpose(0, 3, 1, 2, 4)
    nq = S // SEL_Q_BLOCK
    q_c = qg.reshape(B, nq, SEL_Q_BLOCK, G, HPG, D).transpose(1, 0, 2, 3, 4, 5)
    idx_c = idx.reshape(B, G, nq, SEL_Q_BLOCK, n_sel).transpose(2, 0, 1, 3, 4)
    ok_c = ok.reshape(B, G, nq, SEL_Q_BLOCK, n_sel).transpose(2, 0, 1, 3, 4)
    t_c = jnp.arange(S).reshape(nq, SEL_Q_BLOCK)
    bi = jnp.arange(B)[:, None, None, None]
    gi = jnp.arange(G)[None, :, None, None]
    gi5 = jnp.arange(G)[None, :, None, None, None]
    tbl = rel_bias.reshape(N_BUCKETS, G, HPG)

    def block(args):
        qb, ib, okb, tb = args
        kg = kb[bi, gi, ib]
        vg = vb[bi, gi, ib]
        pos = ib[..., None] * SEL_BLOCK + jnp.arange(SEL_BLOCK)
        dist = tb[None, None, :, None, None] - pos
        mask = (okb[..., None] & (dist >= 0)).reshape(B, G, 1, SEL_Q_BLOCK, n_sel * SEL_BLOCK)
        bias = tbl[_t5_bucket(dist), gi5].transpose(0, 1, 5, 2, 3, 4)
        logits = jnp.einsum('bqghd,bgqnsd->bghqns', qb, kg).astype(jnp.float32) * ATTN_SCALE + bias
        p = _masked_softmax(logits.reshape(B, G, HPG, SEL_Q_BLOCK, n_sel * SEL_BLOCK), mask)
        return jnp.einsum('bghqn,bgqnd->bqghd', p.astype(vg.dtype),
                          vg.reshape(B, G, SEL_Q_BLOCK, n_sel * SEL_BLOCK, D))

    out = lax.map(block, (q_c, idx_c, ok_c, t_c))
    return out.transpose(1, 0, 2, 3, 4, 5).reshape(B, S, G, HPG, D)


def _nsa_window(qg, k, v, rel_bias):
    B, S, G, HPG, D = qg.shape
    nq = S // Q_BLOCK
    span = WINDOW + Q_BLOCK
    kp = jnp.pad(k, ((0, 0), (WINDOW, 0), (0, 0), (0, 0)))
    vp = jnp.pad(v, ((0, 0), (WINDOW, 0), (0, 0), (0, 0)))
    q_c = qg.reshape(B, nq, Q_BLOCK, G, HPG, D).transpose(1, 0, 2, 3, 4, 5)
    tq = jnp.arange(Q_BLOCK)
    sk = jnp.arange(span) - WINDOW
    dist = tq[:, None] - sk[None, :]
    band = (dist >= 0) & (dist < WINDOW)
    bias = rel_bias[_t5_bucket(dist)].reshape(Q_BLOCK, span, G, HPG).transpose(2, 3, 0, 1)

    def block(args):
        i, qb = args
        start = i * Q_BLOCK
        kb = lax.dynamic_slice_in_dim(kp, start, span, axis=1)
        vb = lax.dynamic_slice_in_dim(vp, start, span, axis=1)
        mask = band & ((start + sk) >= 0)[None, :]
        logits = jnp.einsum('bqghd,bkgd->bghqk', qb, kb).astype(jnp.float32) * ATTN_SCALE + bias
        p = _masked_softmax(logits, mask)
        return jnp.einsum('bghqk,bkgd->bqghd', p.astype(vb.dtype), vb)

    out = lax.map(block, (jnp.arange(nq), q_c))
    return out.transpose(1, 0, 2, 3, 4, 5).reshape(B, S, G, HPG, D)


def _nsa(q, k_cmp, v_cmp, k_slc, v_slc, k_win, v_win, gate_logit, rel_bias,
         cmp_pos, cmp_w1, cmp_b1, cmp_w2, cmp_b2):
    B, S, _ = q.shape
    qg = q.reshape(B, S, NSA_GROUPS, NSA_HPG, HEAD_DIM)
    kvs = lambda z: z.reshape(B, S, NSA_GROUPS, HEAD_DIM)
    kc = _compress(kvs(k_cmp), cmp_pos[0], cmp_w1[0], cmp_b1[0], cmp_w2[0], cmp_b2[0])
    vc = _compress(kvs(v_cmp), cmp_pos[1], cmp_w1[1], cmp_b1[1], cmp_w2[1], cmp_b2[1])
    o_cmp, p_cmp = _nsa_compressed(qg, kc, vc, rel_bias)
    idx, ok = _selection_indices(p_cmp, S)
    o_slc = _nsa_selected(qg, kvs(k_slc), kvs(v_slc), idx, ok, rel_bias)
    o_win = _nsa_window(qg, kvs(k_win), kvs(v_win), rel_bias)
    g = jax.nn.sigmoid(gate_logit).reshape(B, S, NSA_GROUPS, NSA_HPG, 3)
    o = g[..., 0:1] * o_cmp + g[..., 1:2] * o_slc + g[..., 2:3] * o_win
    return o.reshape(B, S, NSA_DIM)


def _rwkv7(r, k, v, w_lo, a_lo, g_lo, w0, w2, a0, a2, g2, k_k, k_a, r_k, lnx_g, lnx_b):
    B, S, _ = r.shape
    H, N = RWKV_HEADS, RWKV_HEAD_DIM
    w = -jax.nn.softplus(-(w0 + jnp.tanh(w_lo) @ w2)) - 0.5
    decay = jnp.exp(-jnp.exp(w.astype(jnp.float32)))
    a = jax.nn.sigmoid(a0 + a_lo @ a2)
    g = jax.nn.sigmoid(g_lo) @ g2
    kk = (k * k_k).astype(jnp.float32).reshape(B, S, H, N)
    kk = kk / jnp.maximum(jnp.sqrt(jnp.sum(kk * kk, -1, keepdims=True)), 1e-12)
    k = k * (1.0 + (a - 1.0) * k_a)
    heads = lambda z: z.astype(jnp.float32).reshape(B, S, H, N)
    rh, kh, vh, ah, wh = heads(r), heads(k), heads(v), heads(a), heads(decay)

    def step(state, inp):
        rt, wt, kt, vt, kkt, at = inp
        sa = jnp.einsum('bhvk,bhk->bhv', state, -kkt)
        state = (state * wt[:, :, None, :] + sa[..., None] * (kkt * at)[:, :, None, :]
                 + vt[..., None] * kt[:, :, None, :])
        return state, jnp.einsum('bhvk,bhk->bhv', state, rt)

    seq_first = lambda z: z.transpose(1, 0, 2, 3)
    init = jnp.zeros((B, H, N, N), jnp.float32)
    _, y = lax.scan(step, init, (seq_first(rh), seq_first(wh), seq_first(kh),
                                 seq_first(vh), seq_first(kk), seq_first(ah)))
    y = y.transpose(1, 0, 2, 3)
    mu = y.mean(-1, keepdims=True)
    var = jnp.square(y - mu).mean(-1, keepdims=True)
    y = ((y - mu) * lax.rsqrt(var + RWKV_GN_EPS)).reshape(B, S, RWKV_DIM) * lnx_g + lnx_b
    bonus = (jnp.sum(rh * kh * r_k, -1, keepdims=True) * vh).reshape(B, S, RWKV_DIM)
    return ((y + bonus) * g).astype(r.dtype)


def _peer(x, w_query, sub_keys, u_table, v_table):
    B, S, D = x.shape
    q = (x @ w_query).reshape(B, S, PEER_HEADS, 2, PEER_HALF)
    sc = jnp.einsum('bshpd,hpnd->bshpn', q, sub_keys).astype(jnp.float32)
    s1, i1 = lax.top_k(sc[..., 0, :], PEER_TOPK)
    s2, i2 = lax.top_k(sc[..., 1, :], PEER_TOPK)
    cand = (s1[..., :, None] + s2[..., None, :]).reshape(B, S, PEER_HEADS, PEER_TOPK * PEER_TOPK)
    cand_idx = (i1[..., :, None] * N_KEYS + i2[..., None, :]).reshape(B, S, PEER_HEADS, PEER_TOPK * PEER_TOPK)
    top_s, pos = lax.top_k(cand, PEER_TOPK)
    idx = jnp.take_along_axis(cand_idx, pos, axis=-1)
    gate = jax.nn.softmax(top_s, axis=-1)
    T = B * S
    nb = T // PEER_TOKEN_BLOCK
    xb = x.reshape(nb, PEER_TOKEN_BLOCK, D)
    ib = idx.reshape(nb, PEER_TOKEN_BLOCK, PEER_HEADS * PEER_TOPK)
    gb = gate.reshape(nb, PEER_TOKEN_BLOCK, PEER_HEADS * PEER_TOPK)

    def block(args):
        xt, it, gt = args
        h = jax.nn.gelu(jnp.einsum('ted,td->te', u_table[it], xt).astype(jnp.float32))
        return jnp.einsum('te,ted->td', (gt * h).astype(v_table.dtype), v_table[it])

    out = lax.map(block, (xb, ib, gb))
    return out.reshape(B, S, D).astype(x.dtype)


def _hybrid_layer(x, rel_bias, w_in, token_mu, cmp_pos, cmp_w1, cmp_b1, cmp_w2, cmp_b2,
                  rwkv_w0, rwkv_w2, rwkv_a0, rwkv_a2, rwkv_g2, rwkv_k_k, rwkv_k_a, rwkv_r_k,
                  rwkv_lnx_g, rwkv_lnx_b, w_o_nsa, w_o_rwkv, w_out, ln_mix_g, ln_mix_b,
                  peer_w_query, peer_sub_keys, peer_u, peer_v, ln_ffn_g, ln_ffn_b):
    z = x @ w_in
    q, k_cmp, v_cmp, k_slc, v_slc, k_win, v_win, nsa_gate, rw, merge_gate = jnp.split(z, IN_SPLITS, axis=-1)
    y_nsa = _nsa(q, k_cmp, v_cmp, k_slc, v_slc, k_win, v_win, nsa_gate, rel_bias,
                 cmp_pos, cmp_w1, cmp_b1, cmp_w2, cmp_b2)
    prev = jnp.pad(rw, ((0, 0), (1, 0), (0, 0)))[:, :-1]
    rw = rw + token_mu * (prev - rw)
    r, k, v, w_lo, a_lo, g_lo = jnp.split(rw, RWKV_SPLITS, axis=-1)
    y_rwkv = _rwkv7(r, k, v, w_lo, a_lo, g_lo, rwkv_w0, rwkv_w2, rwkv_a0, rwkv_a2, rwkv_g2,
                    rwkv_k_k, rwkv_k_a, rwkv_r_k, rwkv_lnx_g, rwkv_lnx_b)
    g_nsa, g_rwkv = jnp.split(jax.nn.sigmoid(merge_gate), 2, axis=-1)
    mixed = (g_nsa * (y_nsa @ w_o_nsa) + g_rwkv * (y_rwkv @ w_o_rwkv)) @ w_out
    x = _layer_norm(ALPHA * x + mixed, ln_mix_g, ln_mix_b)
    x = _layer_norm(ALPHA * x + _peer(x, peer_w_query, peer_sub_keys, peer_u, peer_v), ln_ffn_g, ln_ffn_b)
    return x


def setup_inputs(seed: int = 0) -> dict:
    key = jax.random.key(seed)
    ks = iter(jax.random.split(key, 48))
    L, D = DEPTH, D_MODEL

    def nrm(shape, scale):
        return jax.random.normal(next(ks), shape, jnp.float32) * scale

    def unif(shape, lo, hi):
        return jax.random.uniform(next(ks), shape, jnp.float32, lo, hi)

    return {
        "x": nrm((BATCH, SEQ, D), 1.0),
        "ln_in_g": 1.0 + nrm((D,), 0.02),
        "ln_in_b": nrm((D,), 0.02),
        "rel_bias": nrm((N_BUCKETS, NSA_HEADS), 0.1),
        "w_in": nrm((L, D, D_IN), D ** -0.5),
        "token_mu": unif((L, RWKV_COLS), 0.0, 1.0),
        "cmp_pos": nrm((L, 2, CMP_BLOCK, HEAD_DIM), 0.1),
        "cmp_w1": nrm((L, 2, CMP_BLOCK * HEAD_DIM, CMP_HIDDEN), (CMP_BLOCK * HEAD_DIM) ** -0.5),
        "cmp_b1": nrm((L, 2, CMP_HIDDEN), 0.02),
        "cmp_w2": nrm((L, 2, CMP_HIDDEN, HEAD_DIM), CMP_HIDDEN ** -0.5),
        "cmp_b2": nrm((L, 2, HEAD_DIM), 0.02),
        "rwkv_w0": unif((L, RWKV_DIM), -6.5, -1.5),
        "rwkv_w2": nrm((L, LORA_W, RWKV_DIM), 0.5 * LORA_W ** -0.5),
        "rwkv_a0": nrm((L, RWKV_DIM), 0.1),
        "rwkv_a2": nrm((L, LORA_A, RWKV_DIM), LORA_A ** -0.5),
        "rwkv_g2": nrm((L, LORA_G, RWKV_DIM), LORA_G ** -0.5),
        "rwkv_k_k": 1.0 + nrm((L, RWKV_DIM), 0.05),
        "rwkv_k_a": 1.0 + nrm((L, RWKV_DIM), 0.05),
        "rwkv_r_k": nrm((L, RWKV_HEADS, RWKV_HEAD_DIM), 0.1),
        "rwkv_lnx_g": 1.0 + nrm((L, RWKV_DIM), 0.02),
        "rwkv_lnx_b": nrm((L, RWKV_DIM), 0.02),
        "w_o_nsa": nrm((L, NSA_DIM, D), NSA_DIM ** -0.5),
        "w_o_rwkv": nrm((L, RWKV_DIM, D), RWKV_DIM ** -0.5),
        "w_out": nrm((L, D, D), BETA * D ** -0.5),
        "ln_mix_g": 1.0 + nrm((L, D), 0.02),
        "ln_mix_b": nrm((L, D), 0.02),
        "peer_w_query": nrm((L, D, PEER_HEADS * PEER_KEY_DIM), D ** -0.5),
        "peer_sub_keys": nrm((L, PEER_HEADS, 2, N_KEYS, PEER_HALF), PEER_HALF ** -0.5),
        "peer_u": nrm((L, N_EXPERTS, D), D ** -0.5),
        "peer_v": nrm((L, N_EXPERTS, D), BETA * PEER_HEADS ** -0.5),
        "ln_ffn_g": 1.0 + nrm((L, D), 0.02),
        "ln_ffn_b": nrm((L, D), 0.02),
    }


def reference(x, ln_in_g, ln_in_b, rel_bias, w_in, token_mu, cmp_pos, cmp_w1, cmp_b1, cmp_w2, cmp_b2,
              rwkv_w0, rwkv_w2, rwkv_a0, rwkv_a2, rwkv_g2, rwkv_k_k, rwkv_k_a, rwkv_r_k,
              rwkv_lnx_g, rwkv_lnx_b, w_o_nsa, w_o_rwkv, w_out, ln_mix_g, ln_mix_b,
              peer_w_query, peer_sub_keys, peer_u, peer_v, ln_ffn_g, ln_ffn_b):
    h = _layer_norm(x, ln_in_g, ln_in_b)
    for l in range(DEPTH):
        h = _hybrid_layer(h, rel_bias, w_in[l], token_mu[l], cmp_pos[l], cmp_w1[l], cmp_b1[l],
                          cmp_w2[l], cmp_b2[l], rwkv_w0[l], rwkv_w2[l], rwkv_a0[l], rwkv_a2[l],
                          rwkv_g2[l], rwkv_k_k[l], rwkv_k_a[l], rwkv_r_k[l], rwkv_lnx_g[l],
                          rwkv_lnx_b[l], w_o_nsa[l], w_o_rwkv[l], w_out[l], ln_mix_g[l], ln_mix_b[l],
                          peer_w_query[l], peer_sub_keys[l], peer_u[l], peer_v[l],
                          ln_ffn_g[l], ln_ffn_b[l])
    return h
```

```python
import functools
import math

import numpy as np
import jax
import jax.numpy as jnp
from jax import lax
from jax.experimental import pallas as pl
from jax.experimental.pallas import tpu as pltpu

F32 = jnp.float32
BF16 = jnp.bfloat16
HI = lax.Precision.HIGHEST

D_MODEL = 1024
NSA_HEADS = 8
NSA_GROUPS = 2
NSA_HPG = NSA_HEADS // NSA_GROUPS
HEAD_DIM = 64
NSA_DIM = NSA_HEADS * HEAD_DIM
NSA_KV_DIM = NSA_GROUPS * HEAD_DIM
CMP_BLOCK = 32
CMP_STRIDE = 16
CMP_HIDDEN = 256
SEL_BLOCK = 64
SEL_TOPN = 16
WINDOW = 512
Q_BLOCK = 128
FORCE_BONUS = 1.0e4
ATTN_SCALE = HEAD_DIM ** -0.5
RWKV_HEADS = 8
RWKV_HEAD_DIM = 64
RWKV_DIM = RWKV_HEADS * RWKV_HEAD_DIM
LORA_W = 64
LORA_A = 64
LORA_G = 128
RWKV_GN_EPS = 64e-5
N_BUCKETS = 32
MAX_DISTANCE = 128
PEER_HEADS = 8
N_KEYS = 128
N_EXPERTS = N_KEYS * N_KEYS
PEER_HALF = 64
PEER_TOPK = 16
DEPTH = 1
ALPHA = (2.0 * DEPTH) ** 0.25
LN_EPS = 1e-5
NEG = -1e30
RWKV_COLS = 3 * RWKV_DIM + LORA_W + LORA_A + LORA_G
IN_SIZES = (NSA_DIM,) + (NSA_KV_DIM,) * 6 + (3 * NSA_HEADS, RWKV_COLS, 2 * D_MODEL)
IN_SPLITS = tuple(int(c) for c in np.cumsum(IN_SIZES)[:-1])

GATE_PAD = 128
Z_PAD = NSA_DIM + 6 * NSA_KV_DIM + GATE_PAD + RWKV_COLS + 2 * D_MODEL
RWKV_CHUNK = 64
VMEM_LIMIT = 56 * 1024 * 1024


def _params(*sem):
    return pltpu.CompilerParams(dimension_semantics=sem, vmem_limit_bytes=VMEM_LIMIT)


def _dot(a, b, prec=None):
    return jnp.dot(a, b, preferred_element_type=F32, precision=prec)


def _dot_nt(a, b, prec=None):
    return lax.dot_general(a, b, (((1,), (1,)), ((), ())), preferred_element_type=F32, precision=prec)


def _dot_tn(a, b, prec=None):
    return lax.dot_general(a, b, (((0,), (0,)), ((), ())), preferred_element_type=F32, precision=prec)


def _layer_norm(x, g, b):
    mu = jnp.mean(x, axis=-1, keepdims=True)
    xc = x - mu
    var = jnp.mean(xc * xc, axis=-1, keepdims=True)
    return xc * lax.rsqrt(var + LN_EPS) * g + b


def _gelu(x):
    return 0.5 * x * (1.0 + jnp.tanh(math.sqrt(2.0 / math.pi) * (x + 0.044715 * (x * x * x))))


def _sigmoid(x):
    return 1.0 / (1.0 + jnp.exp(-x))


def _iota(shape, axis):
    return lax.broadcasted_iota(jnp.int32, shape, axis)


def _bucket_table(n):
    d = np.arange(n)
    large = 16 + (np.log(np.maximum(d, 1).astype(np.float64) / 16.0) / math.log(MAX_DISTANCE / 16.0) * 16.0).astype(np.int64)
    large = np.minimum(large, N_BUCKETS - 1)
    return np.where(d < 16, d, large).astype(np.int32)


def _bias_tables(rel_bias, S):
    bt = _bucket_table(2 * S + 1024)
    nc = S // CMP_STRIDE
    t = np.arange(S)[:, None]
    c = np.arange(nc)[None, :]
    d_cmp = np.maximum(t - (c * CMP_STRIDE + CMP_BLOCK - 1), 0)
    i = np.arange(Q_BLOCK)[:, None]
    u = np.arange(2 * S - Q_BLOCK)[None, :]
    d_sel = np.maximum(i + (S - Q_BLOCK) - u, 0)
    j = np.arange(WINDOW + Q_BLOCK)[None, :]
    d_win = np.maximum(i + WINDOW - j, 0)
    take = lambda d: jnp.moveaxis(rel_bias[jnp.asarray(bt[d])], -1, 0)
    return take(d_cmp), take(d_sel), take(d_win)


_Z_SEGS = (("q", 0, 512), ("kv", 512, 1280), ("gate", 1280, 1408), ("rw", 1408, 3200), ("mg", 3200, 5248))


def _inproj_body(x_ref, g_ref, b_ref, w_ref, h_ref, q_ref, kv_ref, gate_ref, rw_ref, mg_ref):
    h = _layer_norm(x_ref[...], g_ref[...], b_ref[...])
    h_ref[...] = h
    hb = h.astype(BF16)
    for ref, (_, lo, hi) in zip((q_ref, kv_ref, gate_ref, rw_ref, mg_ref), _Z_SEGS):
        ref[...] = _dot(hb, w_ref[:, lo:hi])


def _inproj(x2, ln_g, ln_b, w_pad):
    T = x2.shape[0]
    TM = 256
    row = lambda n: pl.BlockSpec((TM, n), lambda i: (i, 0))
    full = lambda a: pl.BlockSpec(a.shape, lambda i: (0,) * a.ndim)
    widths = [D_MODEL] + [hi - lo for _, lo, hi in _Z_SEGS]
    return pl.pallas_call(
        _inproj_body,
        grid=(T // TM,),
        in_specs=[row(D_MODEL), full(ln_g), full(ln_b), full(w_pad)],
        out_specs=[row(n) for n in widths],
        out_shape=[jax.ShapeDtypeStruct((T, n), F32) for n in widths],
        compiler_params=_params("arbitrary"),
        name="ln_inproj",
    )(x2, ln_g, ln_b, w_pad)


def _compress_body(x_ref, pos_ref, w1_ref, b1_ref, w2_ref, b2_ref, o_ref):
    x = x_ref[...]
    pos = pos_ref[...]
    half = CMP_STRIDE * HEAD_DIM
    a = _dot((x + pos[:, :half]).astype(BF16), w1_ref[:half, :].astype(BF16))
    b = _dot((x + pos[:, half:]).astype(BF16), w1_ref[half:, :].astype(BF16))
    nxt = pltpu.roll(b, x.shape[0] - 1, 0)
    hid = _gelu(a + nxt + b1_ref[...])
    o_ref[...] = _dot(hid.astype(BF16), w2_ref[...].astype(BF16)) + b2_ref[...]


def _compress(xc, pos, w1, b1, w2, b2):
    _, B, G, NCH, W = xc.shape
    sq = lambda *shape: pl.BlockSpec((None,) + shape, lambda i, b, g: (i,) + (0,) * len(shape))
    return pl.pallas_call(
        _compress_body,
        grid=(2, B, G),
        in_specs=[pl.BlockSpec((None, None, None, NCH, W), lambda i, b, g: (i, b, g, 0, 0)),
                  sq(1, 2 * W), sq(2 * W, CMP_HIDDEN), sq(1, CMP_HIDDEN), sq(CMP_HIDDEN, HEAD_DIM), sq(1, HEAD_DIM)],
        out_specs=pl.BlockSpec((None, None, None, NCH, HEAD_DIM), lambda i, b, g: (i, b, g, 0, 0)),
        out_shape=jax.ShapeDtypeStruct((2, B, G, NCH, HEAD_DIM), F32),
        compiler_params=_params("arbitrary", "arbitrary", "arbitrary"),
        name="nsa_compress",
    )(xc, pos, w1, b1, w2, b2)


CMP_TQ = 256


def _cmp_attn_body(q_ref, kc_ref, vc_ref, bias_ref, ovt_ref, o_ref, sel_ref):
    TQ = CMP_TQ
    t0 = pl.program_id(2) * TQ
    q = q_ref[...]
    kc = kc_ref[...].astype(BF16)
    vc = vc_ref[...].astype(BF16)
    nc = kc.shape[0]
    t = t0 + _iota((TQ, nc), 0)
    c = _iota((TQ, nc), 1)
    mask = (t - (c * CMP_STRIDE + CMP_BLOCK - 1)) >= 0
    psum = jnp.zeros((TQ, nc), F32)
    outs = []
    for h in range(NSA_HPG):
        qh = (q[:, h * HEAD_DIM:(h + 1) * HEAD_DIM] * ATTN_SCALE).astype(BF16)
        lg = jnp.where(mask, _dot_nt(qh, kc) + bias_ref[h], NEG)
        m = jnp.max(lg, axis=-1, keepdims=True)
        e = jnp.exp(lg - m)
        p = jnp.where(mask, e / jnp.sum(e, axis=-1, keepdims=True), 0.0)
        psum = psum + p
        outs.append(_dot(p.astype(BF16), vc))
    o_ref[...] = jnp.concatenate(outs, axis=1)

    nsb = ovt_ref.shape[0]
    imp = _dot_nt(ovt_ref[...], psum, HI)
    j = _iota((nsb, TQ), 0)
    cur = (t0 + _iota((nsb, TQ), 1)) // SEL_BLOCK
    forced = (j == 0) | (j == cur) | (j == cur - 1)
    score = jnp.where(j > cur, NEG, imp + jnp.where(forced, FORCE_BONUS, 0.0))
    rank = jnp.zeros((nsb, TQ), F32)
    for i in range(nsb):
        si = score[i:i + 1, :]
        better = (si > score) | ((si == score) & (j > i))
        rank = rank + jnp.where(better, 1.0, 0.0)
    sel = jnp.where((rank < SEL_TOPN) & (j <= cur), 1.0, 0.0)
    selp = jnp.concatenate([sel, jnp.zeros((128 - nsb, TQ), F32)], axis=0)
    sel_ref[...] = selp.T


def _cmp_attn(q3, kvc, bias_cmp, ovt):
    B, S, _ = q3.shape
    G = NSA_GROUPS
    GW = NSA_HPG * HEAD_DIM
    NCH = kvc.shape[3]
    return pl.pallas_call(
        _cmp_attn_body,
        grid=(B, G, S // CMP_TQ),
        in_specs=[pl.BlockSpec((None, CMP_TQ, GW), lambda b, g, i: (b, i, g)),
                  pl.BlockSpec((None, None, None, NCH, HEAD_DIM), lambda b, g, i: (0, b, g, 0, 0)),
                  pl.BlockSpec((None, None, None, NCH, HEAD_DIM), lambda b, g, i: (1, b, g, 0, 0)),
                  pl.BlockSpec((NSA_HPG, CMP_TQ, NCH), lambda b, g, i: (g, i, 0)),
                  pl.BlockSpec(ovt.shape, lambda b, g, i: (0, 0))],
        out_specs=[pl.BlockSpec((None, CMP_TQ, GW), lambda b, g, i: (b, i, g)),
                   pl.BlockSpec((None, None, CMP_TQ, 128), lambda b, g, i: (b, g, i, 0))],
        out_shape=[jax.ShapeDtypeStruct((B, S, NSA_DIM), F32),
                   jax.ShapeDtypeStruct((B, G, S, 128), F32)],
        compiler_params=_params("arbitrary", "arbitrary", "arbitrary"),
        name="nsa_cmp_select",
    )(q3, kvc, kvc, bias_cmp, ovt)


def _softmax_av(lg, mask, v):
    lg = jnp.where(mask, lg, NEG)
    m = jnp.max(lg, axis=-1, keepdims=True)
    e = jnp.where(mask, jnp.exp(lg - m), 0.0)
    s = jnp.sum(e, axis=-1, keepdims=True)
    return _dot(e.astype(BF16), v) / s


def _nsa_main_body(q_ref, ks_ref, vs_ref, kw_ref, vw_ref, sel_ref, exp_ref, gsel_ref, gwin_ref,
                   gate_ref, oc_ref, o_ref):
    TQ = Q_BLOCK
    S = ks_ref.shape[0]
    qi = pl.program_id(2)
    t0 = pl.multiple_of(qi * TQ, TQ)
    q = q_ref[...]
    q4 = jnp.concatenate([q[:, h * HEAD_DIM:(h + 1) * HEAD_DIM] for h in range(NSA_HPG)], axis=0)
    q4 = (q4 * ATTN_SCALE).astype(BF16)
    rep = lambda m: jnp.concatenate([m] * NSA_HPG, axis=0)

    off = pl.multiple_of((S // TQ - 1 - qi) * TQ, TQ)
    bias = jnp.concatenate([gsel_ref[h, :, pl.ds(off, S)] for h in range(NSA_HPG)], axis=0)
    selm = _dot(sel_ref[...].astype(BF16), exp_ref[...])
    trow = t0 + _iota((TQ, S), 0)
    scol = _iota((TQ, S), 1)
    m_sel = (selm > 0.5) & (scol <= trow)
    o_slc = _softmax_av(_dot_nt(q4, ks_ref[...]) + bias, rep(m_sel), vs_ref[...])

    span = WINDOW + TQ
    kwin = kw_ref[pl.ds(t0, span), :]
    vwin = vw_ref[pl.ds(t0, span), :]
    bias_w = jnp.concatenate([gwin_ref[h] for h in range(NSA_HPG)], axis=0)
    jx = _iota((TQ, span), 1)
    dist = _iota((TQ, span), 0) + WINDOW - jx
    m_win = (dist >= 0) & (dist < WINDOW) & (t0 - WINDOW + jx >= 0)
    o_win = _softmax_av(_dot_nt(q4, kwin) + bias_w, rep(m_win), vwin)

    gs = _sigmoid(gate_ref[...])
    oc = oc_ref[...]
    outs = []
    for h in range(NSA_HPG):
        g0, g1, g2 = (gs[:, 3 * h + r:3 * h + r + 1] for r in range(3))
        outs.append(g0 * oc[:, h * HEAD_DIM:(h + 1) * HEAD_DIM]
                    + g1 * o_slc[h * TQ:(h + 1) * TQ] + g2 * o_win[h * TQ:(h + 1) * TQ])
    o_ref[...] = jnp.concatenate(outs, axis=1)


def _nsa_main(q3, ks, vs, kw, vw, sel, expand, gsel, gwin, gate_g, o_cmp):
    B, S, _ = q3.shape
    G = NSA_GROUPS
    GW = NSA_HPG * HEAD_DIM
    TQ = Q_BLOCK
    SP = kw.shape[2]
    qspec = pl.BlockSpec((None, TQ, GW), lambda g, b, i: (b, i, g))
    kvspec = lambda n: pl.BlockSpec((None, None, n, HEAD_DIM), lambda g, b, i: (b, g, 0, 0))
    tspec = lambda a: pl.BlockSpec((NSA_HPG,) + a.shape[1:], lambda g, b, i: (g, 0, 0))
    tile = pl.BlockSpec((None, None, TQ, 128), lambda g, b, i: (b, g, i, 0))
    return pl.pallas_call(
        _nsa_main_body,
        grid=(G, B, S // TQ),
        in_specs=[qspec, kvspec(S), kvspec(S), kvspec(SP), kvspec(SP), tile,
                  pl.BlockSpec(expand.shape, lambda g, b, i: (0, 0)), tspec(gsel), tspec(gwin), tile, qspec],
        out_specs=qspec,
        out_shape=jax.ShapeDtypeStruct((B, S, NSA_DIM), F32),
        compiler_params=_params("arbitrary", "arbitrary", "arbitrary"),
        name="nsa_select_window",
    )(q3, ks, vs, kw, vw, sel, expand, gsel, gwin, gate_g, o_cmp)


RW_TM = 512


def _rwkv_prep_body(rw_ref, mu_ref, w0_ref, w2_ref, a0_ref, a2_ref, g2_ref, kk_ref, ka_ref, rk_ref,
                    bd_ref, tri_ref, ab_ref, rb_ref, bt_ref, kt_ref, v_ref, pc_ref, g_ref, bonus_ref, carry):
    TM = RW_TM
    D = RWKV_DIM

    @pl.when(pl.program_id(1) == 0)
    def _():
        carry[...] = jnp.zeros_like(carry)

    x = rw_ref[...]
    last = carry[0:1, :]
    prev = jnp.where(_iota(x.shape, 0) == 0, last, pltpu.roll(x, 1, 0))
    carry[0:1, :] = x[TM - 1:TM, :]
    xm = x + mu_ref[...] * (prev - x)
    r, k, v = xm[:, 0:D], xm[:, D:2 * D], xm[:, 2 * D:3 * D]
    lo = xm[:, 3 * D:3 * D + LORA_W + LORA_A]
    g_lo = xm[:, 3 * D + LORA_W + LORA_A:]
    y = -(w0_ref[...] + _dot(jnp.tanh(lo), w2_ref[...], HI))
    softplus = jnp.maximum(y, 0.0) + jnp.log(1.0 + jnp.exp(-jnp.abs(y)))
    logd = -jnp.exp(-softplus - 0.5)
    a = _sigmoid(a0_ref[...] + _dot(lo, a2_ref[...], HI))
    g_ref[...] = _dot(_sigmoid(g_lo), g2_ref[...], HI)
    kk = k * kk_ref[...]
    kk = kk / jnp.maximum(jnp.sqrt(_dot(kk * kk, bd_ref[...], HI)), 1e-12)
    k2 = k * (1.0 + (a - 1.0) * ka_ref[...])
    bonus_ref[...] = _dot(r * k2 * rk_ref[...], bd_ref[...], HI) * v

    cum = _dot(tri_ref[...], logd, HI)
    p_in = jnp.exp(cum)
    p_inv = jnp.exp(-cum)
    ab = -kk * jnp.exp(cum - logd)
    rb = r * p_in
    bt = kk * a * p_inv
    kt = k2 * p_inv
    C = RWKV_CHUNK
    pc = jnp.concatenate([p_in[c * C + C - 1:c * C + C, :] for c in range(TM // C)], axis=0)
    for h in range(RWKV_HEADS):
        sl = slice(h * RWKV_HEAD_DIM, (h + 1) * RWKV_HEAD_DIM)
        ab_ref[h] = ab[:, sl]
        rb_ref[h] = rb[:, sl]
        bt_ref[h] = bt[:, sl]
        kt_ref[h] = kt[:, sl]
        v_ref[h] = v[:, sl]
        pc_ref[h] = pc[:, sl]


def _rwkv_prep(rw3, mu, w0, w2p, a0, a2p, g2, k_k, k_a, r_k, bd, tri):
    B, S, W = rw3.shape
    TM = RW_TM
    H, N = RWKV_HEADS, RWKV_HEAD_DIM
    full = lambda a: pl.BlockSpec(a.shape, lambda b, i: (0,) * a.ndim)
    hm = pl.BlockSpec((None, H, TM, N), lambda b, i: (b, 0, i, 0))
    tok = pl.BlockSpec((None, TM, RWKV_DIM), lambda b, i: (b, i, 0))
    hm_shape = jax.ShapeDtypeStruct((B, H, S, N), F32)
    return pl.pallas_call(
        _rwkv_prep_body,
        grid=(B, S // TM),
        in_specs=[pl.BlockSpec((None, TM, W), lambda b, i: (b, i, 0))]
        + [full(a) for a in (mu, w0, w2p, a0, a2p, g2, k_k, k_a, r_k, bd, tri)],
        out_specs=[hm] * 5 + [pl.BlockSpec((None, H, TM // RWKV_CHUNK, N), lambda b, i: (b, 0, i, 0)), tok, tok],
        out_shape=[hm_shape] * 5 + [jax.ShapeDtypeStruct((B, H, S // RWKV_CHUNK, N), F32),
                                    jax.ShapeDtypeStruct((B, S, RWKV_DIM), F32),
                                    jax.ShapeDtypeStruct((B, S, RWKV_DIM), F32)],
        scratch_shapes=[pltpu.VMEM((8, W), F32)],
        compiler_params=_params("arbitrary", "arbitrary"),
        name="rwkv_prep",
    )(rw3, mu, w0, w2p, a0, a2p, g2, k_k, k_a, r_k, bd, tri)


SCAN_PREC = HI


def _rwkv_scan_body(ab_ref, rb_ref, bt_ref, kt_ref, v_ref, pc_ref, y_ref, s_ref):
    C = RWKV_CHUNK
    P = SCAN_PREC

    @pl.when(pl.program_id(1) == 0)
    def _():
        s_ref[...] = jnp.zeros_like(s_ref)

    row = _iota((C, C), 0)
    col = _iota((C, C), 1)
    strict = row > col
    incl = row >= col
    eye = jnp.where(row == col, 1.0, 0.0)
    for h in range(RWKV_HEADS):
        ab, rb, bt, kt, v = ab_ref[h], rb_ref[h], bt_ref[h], kt_ref[h], v_ref[h]
        s0 = s_ref[h]
        ar = jnp.concatenate([ab, rb], axis=0)
        mb = _dot_nt(ar, bt, P)
        mk = _dot_nt(ar, kt, P)
        l_ab = jnp.where(strict, mb[:C], 0.0)
        m_rb = jnp.where(incl, mb[C:], 0.0)
        l_ak = jnp.where(strict, mk[:C], 0.0)
        m_rk = jnp.where(incl, mk[C:], 0.0)
        inv = eye + l_ab
        lp = l_ab
        for _ in range(int(math.log2(C)) - 1):
            lp = _dot(lp, lp, P)
            inv = inv + _dot(lp, inv, P)
        ars = _dot_nt(ar, s0, P)
        u = _dot(inv, ars[:C] + _dot(l_ak, v, P), P)
        y_ref[h] = ars[C:] + _dot(m_rb, u, P) + _dot(m_rk, v, P)
        uv = jnp.concatenate([u, v], axis=0)
        bk = jnp.concatenate([bt, kt], axis=0)
        s_ref[h] = (s0 + _dot_tn(uv, bk, P)) * pc_ref[h]


def _rwkv_scan(ab, rb, bt, kt, v, pc5):
    B, H, S, N = ab.shape
    C = RWKV_CHUNK
    blk = pl.BlockSpec((None, H, C, N), lambda b, c: (b, 0, c, 0))
    return pl.pallas_call(
        _rwkv_scan_body,
        grid=(B, S // C),
        in_specs=[blk] * 5 + [pl.BlockSpec((None, H, None, 1, N), lambda b, c: (b, 0, c, 0, 0))],
        out_specs=blk,
        out_shape=jax.ShapeDtypeStruct((B, H, S, N), F32),
        scratch_shapes=[pltpu.VMEM((H, N, N), F32)],
        compiler_params=_params("arbitrary", "arbitrary"),
        name="rwkv_scan",
    )(ab, rb, bt, kt, v, pc5)


MIX_TM = 256


def _mix_body(h_ref, yn_ref, y_ref, bonus_ref, g_ref, mg_ref, lg_ref, lb_ref, wn_ref, wr_ref, wo_ref,
              ng_ref, nb_ref, o_ref):
    ys = []
    for h in range(RWKV_HEADS):
        y = y_ref[h]
        mu = jnp.mean(y, axis=-1, keepdims=True)
        yc = y - mu
        var = jnp.mean(yc * yc, axis=-1, keepdims=True)
        ys.append(yc * lax.rsqrt(var + RWKV_GN_EPS))
    y = jnp.concatenate(ys, axis=1) * lg_ref[...] + lb_ref[...]
    y_rwkv = (y + bonus_ref[...]) * g_ref[...]
    gate = _sigmoid(mg_ref[...])
    a = _dot(yn_ref[...].astype(BF16), wn_ref[...])
    b = _dot(y_rwkv.astype(BF16), wr_ref[...])
    m = gate[:, :D_MODEL] * a + gate[:, D_MODEL:] * b
    mixed = _dot(m.astype(BF16), wo_ref[...])
    o_ref[...] = _layer_norm(ALPHA * h_ref[...] + mixed, ng_ref[...], nb_ref[...])


def _mix(h3, y_nsa, y_hm, bonus, g, mg3, lnx_g, lnx_b, wn, wr, wo, ng, nb):
    B, S, D = h3.shape
    TM = MIX_TM
    tok = lambda n: pl.BlockSpec((None, TM, n), lambda b, i: (b, i, 0))
    full = lambda a: pl.BlockSpec(a.shape, lambda b, i: (0,) * a.ndim)
    return pl.pallas_call(
        _mix_body,
        grid=(B, S // TM),
        in_specs=[tok(D), tok(NSA_DIM),
                  pl.BlockSpec((None, RWKV_HEADS, TM, RWKV_HEAD_DIM), lambda b, i: (b, 0, i, 0)),
                  tok(RWKV_DIM), tok(RWKV_DIM), tok(2 * D)]
        + [full(a) for a in (lnx_g, lnx_b, wn, wr, wo, ng, nb)],
        out_specs=tok(D),
        out_shape=jax.ShapeDtypeStruct((B, S, D), F32),
        compiler_params=_params("arbitrary", "arbitrary"),
        name="mixer_merge",
    )(h3, y_nsa, y_hm, bonus, g, mg3, lnx_g, lnx_b, wn, wr, wo, ng, nb)


RT_TM = 256
NO_RANK = 99.0


def _route_body(x_ref, wq_ref, keys_ref, jq_ref, r2_ref, c1_ref, p2_ref):
    TM = RT_TM
    K = PEER_TOPK
    H = PEER_HEADS
    qt = _dot_nt(wq_ref[...], x_ref[...], HI)
    sc, rk, top = {}, {}, {}
    for h in range(H):
        for p in range(2):
            base = (2 * h + p) * PEER_HALF
            s = _dot(keys_ref[h, p], qt[base:base + PEER_HALF, :], HI)
            work = s
            rank = jnp.full(s.shape, NO_RANK, F32)
            vals = []
            for r in range(K):
                m = jnp.max(work, axis=0, keepdims=True)
                hit = work == m
                rank = jnp.where(hit, float(r + 1), rank)
                work = jnp.where(hit, -jnp.inf, work)
                vals.append(m)
            sc[h, p], rk[h, p], top[h, p] = s, rank, vals
    v1 = [jnp.concatenate([top[h, 0][i] for h in range(H)], axis=0) for i in range(K)]
    v2 = [jnp.concatenate([top[h, 1][i] for h in range(H)], axis=0) for i in range(K)]
    pairs = [(i, j) for i in range(K) for j in range(K) if (i + 1) * (j + 1) <= K]
    cand = {ij: v1[ij[0]] + v2[ij[1]] for ij in pairs}
    work = dict(cand)
    tau = None
    for _ in range(K):
        tau = functools.reduce(jnp.maximum, work.values())
        work = {ij: jnp.where(w == tau, -jnp.inf, w) for ij, w in work.items()}
    cmax = cand[0, 0]
    z = jnp.zeros_like(cmax)
    jcnt = [jnp.zeros_like(cmax) for _ in range(K)]
    for (i, j), cv in cand.items():
        keep = cv >= tau
        z = z + jnp.where(keep, jnp.exp(cv - cmax), 0.0)
        jcnt[i] = jcnt[i] + jnp.where(keep, 1.0, 0.0)
    zinv = 1.0 / z
    for h in range(H):
        r1 = rk[h, 0]
        jq = jnp.zeros(r1.shape, F32)
        for i in range(K):
            jq = jnp.where(r1 == float(i + 1), jcnt[i][h:h + 1, :], jq)
        jq_ref[h] = jq
        r2_ref[h] = rk[h, 1]
        c1_ref[h] = jnp.exp(sc[h, 0] - top[h, 0][0]) * zinv[h:h + 1, :]
        p2_ref[h] = jnp.exp(sc[h, 1] - top[h, 1][0])


def _route(x1, wq_t, keys):
    T, D = x1.shape
    TM = RT_TM
    H = PEER_HEADS
    out = pl.BlockSpec((H, N_KEYS, TM), lambda i: (0, 0, i))
    shp = jax.ShapeDtypeStruct((H, N_KEYS, T), F32)
    return pl.pallas_call(
        _route_body,
        grid=(T // TM,),
        in_specs=[pl.BlockSpec((TM, D), lambda i: (i, 0)),
                  pl.BlockSpec(wq_t.shape, lambda i: (0, 0)),
                  pl.BlockSpec(keys.shape, lambda i: (0, 0, 0, 0))],
        out_specs=[out] * 4,
        out_shape=[shp] * 4,
        compiler_params=_params("arbitrary"),
        name="peer_route",
    )(x1, wq_t, keys)


PE_TM = 256
PE_EB = 512


def _peer_body(x_ref, u_ref, vt_ref, jq_ref, r2_ref, c1_ref, p2_ref, ng_ref, nb_ref, o_ref, acc):
    j = pl.program_id(1)
    TM = PE_TM
    groups = PE_EB // N_KEYS

    @pl.when(j == 0)
    def _():
        acc[...] = jnp.zeros_like(acc)

    x = x_ref[...]
    hid = _gelu(_dot_nt(u_ref[...], x.astype(BF16)))
    ws = []
    for aa in range(groups):
        a = j * groups + aa
        w = jnp.zeros((N_KEYS, TM), F32)
        for h in range(PEER_HEADS):
            jq = jq_ref[h, pl.ds(a, 1), :]
            c1 = c1_ref[h, pl.ds(a, 1), :]
            w = w + jnp.where(r2_ref[h] <= jq, p2_ref[h], 0.0) * c1
        ws.append(w)
    gh = (jnp.concatenate(ws, axis=0) * hid).astype(BF16)
    acc[...] += _dot(vt_ref[...], gh)

    @pl.when(j == pl.num_programs(1) - 1)
    def _():
        o_ref[...] = _layer_norm(ALPHA * x + acc[...].T, ng_ref[...], nb_ref[...])


def _peer(x1, u_b, vt_b, jq, r2, c1, p2, ng, nb):
    T, D = x1.shape
    TM, EB = PE_TM, PE_EB
    H = PEER_HEADS
    rt = pl.BlockSpec((H, N_KEYS, TM), lambda i, j: (0, 0, i))
    full = lambda a: pl.BlockSpec(a.shape, lambda i, j: (0,) * a.ndim)
    return pl.pallas_call(
        _peer_body,
        grid=(T // TM, N_EXPERTS // EB),
        in_specs=[pl.BlockSpec((TM, D), lambda i, j: (i, 0)),
                  pl.BlockSpec((EB, D), lambda i, j: (j, 0)),
                  pl.BlockSpec((D, EB), lambda i, j: (0, j)),
                  rt, rt, rt, rt, full(ng), full(nb)],
        out_specs=pl.BlockSpec((TM, D), lambda i, j: (i, 0)),
        out_shape=jax.ShapeDtypeStruct((T, D), F32),
        scratch_shapes=[pltpu.VMEM((D, TM), F32)],
        compiler_params=_params("arbitrary", "arbitrary"),
        name="peer_experts",
    )(x1, u_b, vt_b, jq, r2, c1, p2, ng, nb)


def _static_tables(S):
    nsb = S // SEL_BLOCK
    nc = S // CMP_STRIDE
    c0 = np.arange(nc) * CMP_STRIDE
    s0 = np.arange(nsb) * SEL_BLOCK
    ov = np.clip(np.minimum(c0[:, None] + CMP_BLOCK, s0[None, :] + SEL_BLOCK)
                 - np.maximum(c0[:, None], s0[None, :]), 0, None).astype(np.float32) / CMP_BLOCK
    ov[nc - 1:, :] = 0.0
    expand = np.zeros((128, S), np.float32)
    expand[np.arange(S) // SEL_BLOCK, np.arange(S)] = 1.0
    hid = np.arange(RWKV_DIM) // RWKV_HEAD_DIM
    bd = (hid[:, None] == hid[None, :]).astype(np.float32)
    t = np.arange(RW_TM)
    tri = ((t[:, None] >= t[None, :]) & (t[:, None] // RWKV_CHUNK == t[None, :] // RWKV_CHUNK)).astype(np.float32)
    return jnp.asarray(ov.T), jnp.asarray(expand, BF16), jnp.asarray(bd), jnp.asarray(tri)


def _nsa(q, kv, gate, rel_bias, cmp_pos, cmp_w1, cmp_b1, cmp_w2, cmp_b2, B, S, tables):
    G = NSA_GROUPS
    ovt, expand = tables
    kv5 = kv.reshape(B, S, 6, G, HEAD_DIM)
    chunks = lambda a: a.reshape(B, S // CMP_STRIDE, CMP_STRIDE, G, HEAD_DIM).transpose(0, 3, 1, 2, 4) \
        .reshape(B, G, S // CMP_STRIDE, CMP_STRIDE * HEAD_DIM)
    xc = jnp.stack([chunks(kv5[:, :, 0]), chunks(kv5[:, :, 1])])
    kvc = _compress(xc, cmp_pos.reshape(2, 1, CMP_BLOCK * HEAD_DIM), cmp_w1, cmp_b1.reshape(2, 1, CMP_HIDDEN),
                    cmp_w2, cmp_b2.reshape(2, 1, HEAD_DIM))
    bias_cmp, gsel, gwin = _bias_tables(rel_bias, S)
    q3 = q.reshape(B, S, NSA_DIM)
    o_cmp, sel = _cmp_attn(q3, kvc, bias_cmp, ovt)
    heads_first = lambda a: a.transpose(0, 2, 1, 3).astype(BF16)
    ks, vs = heads_first(kv5[:, :, 2]), heads_first(kv5[:, :, 3])
    padw = lambda a: jnp.pad(heads_first(a), ((0, 0), (0, 0), (WINDOW, 0), (0, 0)))
    kw, vw = padw(kv5[:, :, 4]), padw(kv5[:, :, 5])
    gate_g = gate.reshape(B, S, GATE_PAD)[:, :, :3 * NSA_HEADS].reshape(B, S, G, 3 * NSA_HPG).transpose(0, 2, 1, 3)
    gate_g = jnp.pad(gate_g, ((0, 0), (0, 0), (0, 0), (0, 128 - 3 * NSA_HPG)))
    return _nsa_main(q3, ks, vs, kw, vw, sel, expand, gsel, gwin, gate_g, o_cmp)


def kernel(x, ln_in_g, ln_in_b, rel_bias, w_in, token_mu, cmp_pos, cmp_w1, cmp_b1, cmp_w2, cmp_b2, rwkv_w0, rwkv_w2, rwkv_a0, rwkv_a2, rwkv_g2, rwkv_k_k, rwkv_k_a, rwkv_r_k, rwkv_lnx_g, rwkv_lnx_b, w_o_nsa, w_o_rwkv, w_out, ln_mix_g, ln_mix_b, peer_w_query, peer_sub_keys, peer_u, peer_v, ln_ffn_g, ln_ffn_b):
    B, S, D = x.shape
    T = B * S
    row = lambda a: a.reshape(1, -1)
    ovt, expand, bd, tri = _static_tables(S)

    segs = jnp.split(w_in[0], IN_SPLITS, axis=1)
    w_pad = jnp.concatenate(segs[:7] + [jnp.pad(segs[7], ((0, 0), (0, GATE_PAD - 3 * NSA_HEADS)))] + segs[8:],
                            axis=1).astype(BF16)
    h, q, kv, gate, rw, mg = _inproj(x.reshape(T, D), row(ln_in_g), row(ln_in_b), w_pad)

    y_nsa = _nsa(q, kv, gate, rel_bias, cmp_pos[0], cmp_w1[0], cmp_b1[0], cmp_w2[0], cmp_b2[0], B, S, (ovt, expand))

    zpad = jnp.zeros((LORA_W, RWKV_DIM), F32)
    w2p = jnp.concatenate([rwkv_w2[0], zpad], axis=0)
    a2p = jnp.concatenate([zpad, rwkv_a2[0]], axis=0)
    ab, rb, bt, kt, v, pc, g, bonus = _rwkv_prep(
        rw.reshape(B, S, RWKV_COLS), row(token_mu[0]), row(rwkv_w0[0]), w2p, row(rwkv_a0[0]), a2p, rwkv_g2[0],
        row(rwkv_k_k[0]), row(rwkv_k_a[0]), row(rwkv_r_k[0]), bd, tri)
    y_hm = _rwkv_scan(ab, rb, bt, kt, v, pc.reshape(B, RWKV_HEADS, S // RWKV_CHUNK, 1, RWKV_HEAD_DIM))

    x1 = _mix(h.reshape(B, S, D), y_nsa, y_hm, bonus, g, mg.reshape(B, S, 2 * D), row(rwkv_lnx_g[0]),
              row(rwkv_lnx_b[0]), w_o_nsa[0].astype(BF16), w_o_rwkv[0].astype(BF16), w_out[0].astype(BF16),
              row(ln_mix_g[0]), row(ln_mix_b[0])).reshape(T, D)

    jq, r2, c1, p2 = _route(x1, peer_w_query[0].T, peer_sub_keys[0])
    out = _peer(x1, peer_u[0].astype(BF16), peer_v[0].T.astype(BF16), jq, r2, c1, p2,
                row(ln_ffn_g[0]), row(ln_ffn_b[0]))
    return out.reshape(B, S, D)
```

```python
import functools
import math

import numpy as np
import jax
import jax.numpy as jnp
from jax import lax
from jax.experimental import pallas as pl
from jax.experimental.pallas import tpu as pltpu

F32 = jnp.float32
BF16 = jnp.bfloat16
HI = lax.Precision.HIGHEST

D_MODEL = 1024
NSA_HEADS = 8
NSA_GROUPS = 2
NSA_HPG = NSA_HEADS // NSA_GROUPS
HEAD_DIM = 64
NSA_DIM = NSA_HEADS * HEAD_DIM
NSA_KV_DIM = NSA_GROUPS * HEAD_DIM
CMP_BLOCK = 32
CMP_STRIDE = 16
CMP_HIDDEN = 256
SEL_BLOCK = 64
SEL_TOPN = 16
WINDOW = 512
Q_BLOCK = 128
FORCE_BONUS = 1.0e4
ATTN_SCALE = HEAD_DIM ** -0.5
RWKV_HEADS = 8
RWKV_HEAD_DIM = 64
RWKV_DIM = RWKV_HEADS * RWKV_HEAD_DIM
LORA_W = 64
LORA_A = 64
LORA_G = 128
RWKV_GN_EPS = 64e-5
N_BUCKETS = 32
MAX_DISTANCE = 128
PEER_HEADS = 8
N_KEYS = 128
N_EXPERTS = N_KEYS * N_KEYS
PEER_HALF = 64
PEER_TOPK = 16
DEPTH = 1
ALPHA = (2.0 * DEPTH) ** 0.25
LN_EPS = 1e-5
NEG = -1e30
RWKV_COLS = 3 * RWKV_DIM + LORA_W + LORA_A + LORA_G
IN_SIZES = (NSA_DIM,) + (NSA_KV_DIM,) * 6 + (3 * NSA_HEADS, RWKV_COLS, 2 * D_MODEL)
IN_SPLITS = tuple(int(c) for c in np.cumsum(IN_SIZES)[:-1])

GATE_PAD = 128
Z_PAD = NSA_DIM + 6 * NSA_KV_DIM + GATE_PAD + RWKV_COLS + 2 * D_MODEL
RWKV_CHUNK = 64
VMEM_LIMIT = 56 * 1024 * 1024


def _params(*sem):
    return pltpu.CompilerParams(dimension_semantics=sem, vmem_limit_bytes=VMEM_LIMIT)


def _dot(a, b, prec=None):
    return jnp.dot(a, b, preferred_element_type=F32, precision=prec)


def _dot_nt(a, b, prec=None):
    return lax.dot_general(a, b, (((1,), (1,)), ((), ())), preferred_element_type=F32, precision=prec)


def _dot_tn(a, b, prec=None):
    return lax.dot_general(a, b, (((0,), (0,)), ((), ())), preferred_element_type=F32, precision=prec)


def _layer_norm(x, g, b):
    mu = jnp.mean(x, axis=-1, keepdims=True)
    xc = x - mu
    var = jnp.mean(xc * xc, axis=-1, keepdims=True)
    return xc * lax.rsqrt(var + LN_EPS) * g + b


def _gelu(x):
    return 0.5 * x * (1.0 + jnp.tanh(math.sqrt(2.0 / math.pi) * (x + 0.044715 * (x * x * x))))


def _sigmoid(x):
    return 1.0 / (1.0 + jnp.exp(-x))


def _iota(shape, axis):
    return lax.broadcasted_iota(jnp.int32, shape, axis)


def _bucket_table(n):
    d = np.arange(n)
    large = 16 + (np.log(np.maximum(d, 1).astype(np.float64) / 16.0) / math.log(MAX_DISTANCE / 16.0) * 16.0).astype(np.int64)
    large = np.minimum(large, N_BUCKETS - 1)
    return np.where(d < 16, d, large).astype(np.int32)


def _bucket_thresholds():
    bt = _bucket_table(4096)
    thr = [int(np.argmax(bt >= b)) for b in range(17, N_BUCKETS)]
    rebuilt = np.where(np.arange(4096) < 16, np.arange(4096), 16 + sum((np.arange(4096) >= t).astype(np.int64) for t in thr))
    assert (rebuilt == bt).all()
    return thr


def _bias_of_distance(rb_ref, dist, h, delta):
    d = jnp.maximum(dist, 0)
    b = jnp.where(d < 16, d, 16)
    for thr in _bucket_thresholds():
        b = b + jnp.where(d >= thr, 1, 0)
    far = rb_ref[N_BUCKETS - 1, h]
    val = jnp.full(d.shape, far, F32)
    for k in range(N_BUCKETS - 1):
        val = jnp.where(b == k, rb_ref[k, h], val)
    return val - far if delta else val


CMP_BT = 256


def _bias_cmp_body(rb_ref, o_ref):
    t = pl.program_id(0) * CMP_BT + _iota(o_ref.shape[1:], 0)
    c = _iota(o_ref.shape[1:], 1)
    dist = t - (c * CMP_STRIDE + CMP_BLOCK - 1)
    for h in range(NSA_HEADS):
        o_ref[h] = _bias_of_distance(rb_ref, dist, h, False)


def _bias_near_body(rb_ref, dsel_ref, dwin_ref):
    for ref, lead in ((dsel_ref, Q_BLOCK), (dwin_ref, WINDOW)):
        dist = _iota(ref.shape[1:], 0) + lead - _iota(ref.shape[1:], 1)
        for h in range(NSA_HEADS):
            ref[h] = _bias_of_distance(rb_ref, dist, h, True)


def _bias_tables(rel_bias, S):
    nc = S // CMP_STRIDE
    smem = pl.BlockSpec(memory_space=pltpu.SMEM)
    bias_cmp = pl.pallas_call(
        _bias_cmp_body,
        grid=(S // CMP_BT,),
        in_specs=[smem],
        out_specs=pl.BlockSpec((NSA_HEADS, CMP_BT, nc), lambda i: (0, i, 0)),
        out_shape=jax.ShapeDtypeStruct((NSA_HEADS, S, nc), F32),
        compiler_params=_params("arbitrary"),
        name="nsa_bias_cmp",
    )(rel_bias)
    dsel, dwin = pl.pallas_call(
        _bias_near_body,
        in_specs=[smem],
        out_shape=[jax.ShapeDtypeStruct((NSA_HEADS, Q_BLOCK, 2 * Q_BLOCK), F32),
                   jax.ShapeDtypeStruct((NSA_HEADS, Q_BLOCK, WINDOW + Q_BLOCK), F32)],
        name="nsa_bias_near",
    )(rel_bias)
    return bias_cmp, dsel, dwin


_Z_SEGS = (("q", 0, 512), ("kv", 512, 1280), ("gate", 1280, 1408), ("rw", 1408, 3200), ("mg", 3200, 5248))


def _inproj_body(x_ref, g_ref, b_ref, w_ref, h_ref, q_ref, kv_ref, gate_ref, rw_ref, mg_ref):
    h = _layer_norm(x_ref[...], g_ref[...], b_ref[...])
    h_ref[...] = h
    hb = h.astype(BF16)
    for ref, (_, lo, hi) in zip((q_ref, kv_ref, gate_ref, rw_ref, mg_ref), _Z_SEGS):
        ref[...] = _dot(hb, w_ref[:, lo:hi])


def _inproj(x2, ln_g, ln_b, w_pad):
    T = x2.shape[0]
    TM = 256
    row = lambda n: pl.BlockSpec((TM, n), lambda i: (i, 0))
    full = lambda a: pl.BlockSpec(a.shape, lambda i: (0,) * a.ndim)
    widths = [D_MODEL] + [hi - lo for _, lo, hi in _Z_SEGS]
    return pl.pallas_call(
        _inproj_body,
        grid=(T // TM,),
        in_specs=[row(D_MODEL), full(ln_g), full(ln_b), full(w_pad)],
        out_specs=[row(n) for n in widths],
        out_shape=[jax.ShapeDtypeStruct((T, n), F32) for n in widths],
        compiler_params=_params("arbitrary"),
        name="ln_inproj",
    )(x2, ln_g, ln_b, w_pad)


def _compress_body(x_ref, pos_ref, w1_ref, b1_ref, w2_ref, b2_ref, o_ref):
    x = x_ref[...]
    pos = pos_ref[...]
    half = CMP_STRIDE * HEAD_DIM
    a = _dot((x + pos[:, :half]).astype(BF16), w1_ref[:half, :].astype(BF16))
    b = _dot((x + pos[:, half:]).astype(BF16), w1_ref[half:, :].astype(BF16))
    nxt = pltpu.roll(b, x.shape[0] - 1, 0)
    hid = _gelu(a + nxt + b1_ref[...])
    o_ref[...] = _dot(hid.astype(BF16), w2_ref[...].astype(BF16)) + b2_ref[...]


def _compress(xc, pos, w1, b1, w2, b2):
    _, B, G, NCH, W = xc.shape
    sq = lambda *shape: pl.BlockSpec((None,) + shape, lambda i, b, g: (i,) + (0,) * len(shape))
    return pl.pallas_call(
        _compress_body,
        grid=(2, B, G),
        in_specs=[pl.BlockSpec((None, None, None, NCH, W), lambda i, b, g: (i, b, g, 0, 0)),
                  sq(1, 2 * W), sq(2 * W, CMP_HIDDEN), sq(1, CMP_HIDDEN), sq(CMP_HIDDEN, HEAD_DIM), sq(1, HEAD_DIM)],
        out_specs=pl.BlockSpec((None, None, None, NCH, HEAD_DIM), lambda i, b, g: (i, b, g, 0, 0)),
        out_shape=jax.ShapeDtypeStruct((2, B, G, NCH, HEAD_DIM), F32),
        compiler_params=_params("arbitrary", "arbitrary", "arbitrary"),
        name="nsa_compress",
    )(xc, pos, w1, b1, w2, b2)


CMP_TQ = 256


def _cmp_attn_body(q_ref, kc_ref, vc_ref, bias_ref, ovt_ref, o_ref, sel_ref):
    TQ = CMP_TQ
    t0 = pl.program_id(2) * TQ
    q = q_ref[...]
    kc = kc_ref[...].astype(BF16)
    vc = vc_ref[...].astype(BF16)
    nc = kc.shape[0]
    t = t0 + _iota((TQ, nc), 0)
    c = _iota((TQ, nc), 1)
    mask = (t - (c * CMP_STRIDE + CMP_BLOCK - 1)) >= 0
    psum = jnp.zeros((TQ, nc), F32)
    outs = []
    for h in range(NSA_HPG):
        qh = (q[:, h * HEAD_DIM:(h + 1) * HEAD_DIM] * ATTN_SCALE).astype(BF16)
        lg = jnp.where(mask, _dot_nt(qh, kc) + bias_ref[h], NEG)
        m = jnp.max(lg, axis=-1, keepdims=True)
        e = jnp.exp(lg - m)
        p = jnp.where(mask, e / jnp.sum(e, axis=-1, keepdims=True), 0.0)
        psum = psum + p
        outs.append(_dot(p.astype(BF16), vc))
    o_ref[...] = jnp.concatenate(outs, axis=1)

    nsb = ovt_ref.shape[0]
    imp = _dot_nt(ovt_ref[...], psum, HI)
    j = _iota((nsb, TQ), 0)
    cur = (t0 + _iota((nsb, TQ), 1)) // SEL_BLOCK
    forced = (j == 0) | (j == cur) | (j == cur - 1)
    score = jnp.where(j > cur, NEG, imp + jnp.where(forced, FORCE_BONUS, 0.0))
    rank = jnp.zeros((nsb, TQ), F32)
    for i in range(nsb):
        si = score[i:i + 1, :]
        better = (si > score) | ((si == score) & (j > i))
        rank = rank + jnp.where(better, 1.0, 0.0)
    sel = jnp.where((rank < SEL_TOPN) & (j <= cur), 1.0, 0.0)
    selp = jnp.concatenate([sel, jnp.zeros((128 - nsb, TQ), F32)], axis=0)
    sel_ref[...] = selp.T


def _cmp_attn(q3, kvc, bias_cmp, ovt):
    B, S, _ = q3.shape
    G = NSA_GROUPS
    GW = NSA_HPG * HEAD_DIM
    NCH = kvc.shape[3]
    return pl.pallas_call(
        _cmp_attn_body,
        grid=(B, G, S // CMP_TQ),
        in_specs=[pl.BlockSpec((None, CMP_TQ, GW), lambda b, g, i: (b, i, g)),
                  pl.BlockSpec((None, None, None, NCH, HEAD_DIM), lambda b, g, i: (0, b, g, 0, 0)),
                  pl.BlockSpec((None, None, None, NCH, HEAD_DIM), lambda b, g, i: (1, b, g, 0, 0)),
                  pl.BlockSpec((NSA_HPG, CMP_TQ, NCH), lambda b, g, i: (g, i, 0)),
                  pl.BlockSpec(ovt.shape, lambda b, g, i: (0, 0))],
        out_specs=[pl.BlockSpec((None, CMP_TQ, GW), lambda b, g, i: (b, i, g)),
                   pl.BlockSpec((None, None, CMP_TQ, 128), lambda b, g, i: (b, g, i, 0))],
        out_shape=[jax.ShapeDtypeStruct((B, S, NSA_DIM), F32),
                   jax.ShapeDtypeStruct((B, G, S, 128), F32)],
        compiler_params=_params("arbitrary", "arbitrary", "arbitrary"),
        name="nsa_cmp_select",
    )(q3, kvc, kvc, bias_cmp, ovt)


def _softmax_av(lg, mask, v):
    lg = jnp.where(mask, lg, NEG)
    m = jnp.max(lg, axis=-1, keepdims=True)
    e = jnp.where(mask, jnp.exp(lg - m), 0.0)
    s = jnp.sum(e, axis=-1, keepdims=True)
    return _dot(e.astype(BF16), v) / s


def _nsa_main_body(q_ref, ks_ref, vs_ref, kw_ref, vw_ref, sel_ref, exp_ref, gsel_ref, gwin_ref,
                   gate_ref, oc_ref, o_ref, lg_ref):
    TQ = Q_BLOCK
    S = ks_ref.shape[0]
    qi = pl.program_id(2)
    t0 = pl.multiple_of(qi * TQ, TQ)
    q = q_ref[...]
    q4 = jnp.concatenate([q[:, h * HEAD_DIM:(h + 1) * HEAD_DIM] for h in range(NSA_HPG)], axis=0)
    q4 = (q4 * ATTN_SCALE).astype(BF16)
    rep = lambda m: jnp.concatenate([m] * NSA_HPG, axis=0)

    lg_ref[...] = _dot_nt(q4, ks_ref[...])
    near = jnp.concatenate([gsel_ref[h] for h in range(NSA_HPG)], axis=0)

    @pl.when(qi > 0)
    def _():
        lo = pl.multiple_of(t0 - TQ, TQ)
        lg_ref[:, pl.ds(lo, 2 * TQ)] = lg_ref[:, pl.ds(lo, 2 * TQ)] + near

    @pl.when(qi == 0)
    def _():
        lg_ref[:, 0:TQ] = lg_ref[:, 0:TQ] + near[:, TQ:]

    selm = _dot(sel_ref[...].astype(BF16), exp_ref[...])
    trow = t0 + _iota((TQ, S), 0)
    scol = _iota((TQ, S), 1)
    m_sel = (selm > 0.5) & (scol <= trow)
    o_slc = _softmax_av(lg_ref[...], rep(m_sel), vs_ref[...])

    span = WINDOW + TQ
    kwin = kw_ref[pl.ds(t0, span), :]
    vwin = vw_ref[pl.ds(t0, span), :]
    bias_w = jnp.concatenate([gwin_ref[h] for h in range(NSA_HPG)], axis=0)
    jx = _iota((TQ, span), 1)
    dist = _iota((TQ, span), 0) + WINDOW - jx
    m_win = (dist >= 0) & (dist < WINDOW) & (t0 - WINDOW + jx >= 0)
    o_win = _softmax_av(_dot_nt(q4, kwin) + bias_w, rep(m_win), vwin)

    gs = _sigmoid(gate_ref[...])
    oc = oc_ref[...]
    outs = []
    for h in range(NSA_HPG):
        g0, g1, g2 = (gs[:, 3 * h + r:3 * h + r + 1] for r in range(3))
        outs.append(g0 * oc[:, h * HEAD_DIM:(h + 1) * HEAD_DIM]
                    + g1 * o_slc[h * TQ:(h + 1) * TQ] + g2 * o_win[h * TQ:(h + 1) * TQ])
    o_ref[...] = jnp.concatenate(outs, axis=1)


def _nsa_main(q3, ks, vs, kw, vw, sel, expand, gsel, gwin, gate_g, o_cmp):
    B, S, _ = q3.shape
    G = NSA_GROUPS
    GW = NSA_HPG * HEAD_DIM
    TQ = Q_BLOCK
    SP = kw.shape[2]
    qspec = pl.BlockSpec((None, TQ, GW), lambda g, b, i: (b, i, g))
    kvspec = lambda n: pl.BlockSpec((None, None, n, HEAD_DIM), lambda g, b, i: (b, g, 0, 0))
    tspec = lambda a: pl.BlockSpec((NSA_HPG,) + a.shape[1:], lambda g, b, i: (g, 0, 0))
    tile = pl.BlockSpec((None, None, TQ, 128), lambda g, b, i: (b, g, i, 0))
    return pl.pallas_call(
        _nsa_main_body,
        grid=(G, B, S // TQ),
        in_specs=[qspec, kvspec(S), kvspec(S), kvspec(SP), kvspec(SP), tile,
                  pl.BlockSpec(expand.shape, lambda g, b, i: (0, 0)), tspec(gsel), tspec(gwin), tile, qspec],
        out_specs=qspec,
        out_shape=jax.ShapeDtypeStruct((B, S, NSA_DIM), F32),
        scratch_shapes=[pltpu.VMEM((NSA_HPG * TQ, S), F32)],
        compiler_params=_params("arbitrary", "arbitrary", "arbitrary"),
        name="nsa_select_window",
    )(q3, ks, vs, kw, vw, sel, expand, gsel, gwin, gate_g, o_cmp)


RW_TM = 512


def _rwkv_prep_body(rw_ref, mu_ref, w0_ref, w2_ref, a0_ref, a2_ref, g2_ref, kk_ref, ka_ref, rk_ref,
                    bd_ref, tri_ref, ab_ref, rb_ref, bt_ref, kt_ref, v_ref, pc_ref, g_ref, bonus_ref, carry):
    TM = RW_TM
    D = RWKV_DIM

    @pl.when(pl.program_id(1) == 0)
    def _():
        carry[...] = jnp.zeros_like(carry)

    x = rw_ref[...]
    last = carry[0:1, :]
    prev = jnp.where(_iota(x.shape, 0) == 0, last, pltpu.roll(x, 1, 0))
    carry[0:1, :] = x[TM - 1:TM, :]
    xm = x + mu_ref[...] * (prev - x)
    r, k, v = xm[:, 0:D], xm[:, D:2 * D], xm[:, 2 * D:3 * D]
    lo = xm[:, 3 * D:3 * D + LORA_W + LORA_A]
    g_lo = xm[:, 3 * D + LORA_W + LORA_A:]
    y = -(w0_ref[...] + _dot(jnp.tanh(lo), w2_ref[...], HI))
    softplus = jnp.maximum(y, 0.0) + jnp.log(1.0 + jnp.exp(-jnp.abs(y)))
    logd = -jnp.exp(-softplus - 0.5)
    a = _sigmoid(a0_ref[...] + _dot(lo, a2_ref[...], HI))
    g_ref[...] = _dot(_sigmoid(g_lo), g2_ref[...], HI)
    kk = k * kk_ref[...]
    kk = kk / jnp.maximum(jnp.sqrt(_dot(kk * kk, bd_ref[...], HI)), 1e-12)
    k2 = k * (1.0 + (a - 1.0) * ka_ref[...])
    bonus_ref[...] = _dot(r * k2 * rk_ref[...], bd_ref[...], HI) * v

    cum = _dot(tri_ref[...], logd, HI)
    p_in = jnp.exp(cum)
    p_inv = jnp.exp(-cum)
    ab = -kk * jnp.exp(cum - logd)
    rb = r * p_in
    bt = kk * a * p_inv
    kt = k2 * p_inv
    C = RWKV_CHUNK
    pc = jnp.concatenate([p_in[c * C + C - 1:c * C + C, :] for c in range(TM // C)], axis=0)
    for h in range(RWKV_HEADS):
        sl = slice(h * RWKV_HEAD_DIM, (h + 1) * RWKV_HEAD_DIM)
        ab_ref[h] = ab[:, sl]
        rb_ref[h] = rb[:, sl]
        bt_ref[h] = bt[:, sl]
        kt_ref[h] = kt[:, sl]
        v_ref[h] = v[:, sl]
        pc_ref[h] = pc[:, sl]


def _rwkv_prep(rw3, mu, w0, w2p, a0, a2p, g2, k_k, k_a, r_k, bd, tri):
    B, S, W = rw3.shape
    TM = RW_TM
    H, N = RWKV_HEADS, RWKV_HEAD_DIM
    full = lambda a: pl.BlockSpec(a.shape, lambda b, i: (0,) * a.ndim)
    hm = pl.BlockSpec((None, H, TM, N), lambda b, i: (b, 0, i, 0))
    tok = pl.BlockSpec((None, TM, RWKV_DIM), lambda b, i: (b, i, 0))
    hm_shape = jax.ShapeDtypeStruct((B, H, S, N), F32)
    return pl.pallas_call(
        _rwkv_prep_body,
        grid=(B, S // TM),
        in_specs=[pl.BlockSpec((None, TM, W), lambda b, i: (b, i, 0))]
        + [full(a) for a in (mu, w0, w2p, a0, a2p, g2, k_k, k_a, r_k, bd, tri)],
        out_specs=[hm] * 5 + [pl.BlockSpec((None, H, TM // RWKV_CHUNK, N), lambda b, i: (b, 0, i, 0)), tok, tok],
        out_shape=[hm_shape] * 5 + [jax.ShapeDtypeStruct((B, H, S // RWKV_CHUNK, N), F32),
                                    jax.ShapeDtypeStruct((B, S, RWKV_DIM), F32),
                                    jax.ShapeDtypeStruct((B, S, RWKV_DIM), F32)],
        scratch_shapes=[pltpu.VMEM((8, W), F32)],
        compiler_params=_params("arbitrary", "arbitrary"),
        name="rwkv_prep",
    )(rw3, mu, w0, w2p, a0, a2p, g2, k_k, k_a, r_k, bd, tri)


def _rwkv_scan_body(ab_ref, rb_ref, bt_ref, kt_ref, v_ref, pc_ref, y_ref, s_ref):
    C = RWKV_CHUNK
    bf = lambda a: a.astype(BF16)

    @pl.when(pl.program_id(1) == 0)
    def _():
        s_ref[...] = jnp.zeros_like(s_ref)

    row = _iota((C, 2 * C), 0)
    col = _iota((C, 2 * C), 1)
    colm = jnp.where(col >= C, col - C, col)
    strict = row > colm
    incl = row >= colm
    left = col < C
    eye_right = jnp.where(col - C == row, 1.0, 0.0)
    zeros = jnp.zeros((C, RWKV_HEAD_DIM), F32)
    H = range(RWKV_HEADS)
    v = [v_ref[h] for h in H]
    s0 = [s_ref[h] for h in H]
    ar = [bf(jnp.concatenate([ab_ref[h], rb_ref[h]], axis=0)) for h in H]
    bk = [bf(jnp.concatenate([bt_ref[h], kt_ref[h]], axis=0)) for h in H]
    m = [_dot_nt(ar[h], bk[h]) for h in H]
    ars = [_dot_nt(ar[h], bf(s0[h])) for h in H]
    lcat = [jnp.where(strict, m[h][:C], 0.0) for h in H]
    mcat = [bf(jnp.where(incl, m[h][C:], 0.0)) for h in H]
    lv = [_dot(bf(lcat[h]), bf(jnp.concatenate([zeros, v[h]], axis=0))) for h in H]
    pi = [jnp.where(left, lcat[h], eye_right) for h in H]
    for _ in range(int(math.log2(C))):
        r = [_dot(bf(pi[h][:, :C]), bf(pi[h])) for h in H]
        pi = [jnp.where(left, r[h], pi[h] + r[h]) for h in H]
    u = [_dot(bf(pi[h]), bf(jnp.concatenate([zeros, ars[h][:C] + lv[h]], axis=0))) for h in H]
    uv = [bf(jnp.concatenate([u[h], v[h]], axis=0)) for h in H]
    ys = [ars[h][C:] + _dot(mcat[h], uv[h]) for h in H]
    sn = [(s0[h] + _dot_tn(uv[h], bk[h])) * pc_ref[h] for h in H]
    for h in H:
        y_ref[h] = ys[h]
        s_ref[h] = sn[h]


def _rwkv_scan(ab, rb, bt, kt, v, pc5):
    B, H, S, N = ab.shape
    C = RWKV_CHUNK
    blk = pl.BlockSpec((None, H, C, N), lambda b, c: (b, 0, c, 0))
    return pl.pallas_call(
        _rwkv_scan_body,
        grid=(B, S // C),
        in_specs=[blk] * 5 + [pl.BlockSpec((None, H, None, 1, N), lambda b, c: (b, 0, c, 0, 0))],
        out_specs=blk,
        out_shape=jax.ShapeDtypeStruct((B, H, S, N), F32),
        scratch_shapes=[pltpu.VMEM((H, N, N), F32)],
        compiler_params=_params("arbitrary", "arbitrary"),
        name="rwkv_scan",
    )(ab, rb, bt, kt, v, pc5)


MIX_TM = 256


def _mix_body(h_ref, yn_ref, y_ref, bonus_ref, g_ref, mg_ref, lg_ref, lb_ref, wn_ref, wr_ref, wo_ref,
              ng_ref, nb_ref, o_ref):
    ys = []
    for h in range(RWKV_HEADS):
        y = y_ref[h]
        mu = jnp.mean(y, axis=-1, keepdims=True)
        yc = y - mu
        var = jnp.mean(yc * yc, axis=-1, keepdims=True)
        ys.append(yc * lax.rsqrt(var + RWKV_GN_EPS))
    y = jnp.concatenate(ys, axis=1) * lg_ref[...] + lb_ref[...]
    y_rwkv = (y + bonus_ref[...]) * g_ref[...]
    gate = _sigmoid(mg_ref[...])
    a = _dot(yn_ref[...].astype(BF16), wn_ref[...])
    b = _dot(y_rwkv.astype(BF16), wr_ref[...])
    m = gate[:, :D_MODEL] * a + gate[:, D_MODEL:] * b
    mixed = _dot(m.astype(BF16), wo_ref[...])
    o_ref[...] = _layer_norm(ALPHA * h_ref[...] + mixed, ng_ref[...], nb_ref[...])


def _mix(h3, y_nsa, y_hm, bonus, g, mg3, lnx_g, lnx_b, wn, wr, wo, ng, nb):
    B, S, D = h3.shape
    TM = MIX_TM
    tok = lambda n: pl.BlockSpec((None, TM, n), lambda b, i: (b, i, 0))
    full = lambda a: pl.BlockSpec(a.shape, lambda b, i: (0,) * a.ndim)
    return pl.pallas_call(
        _mix_body,
        grid=(B, S // TM),
        in_specs=[tok(D), tok(NSA_DIM),
                  pl.BlockSpec((None, RWKV_HEADS, TM, RWKV_HEAD_DIM), lambda b, i: (b, 0, i, 0)),
                  tok(RWKV_DIM), tok(RWKV_DIM), tok(2 * D)]
        + [full(a) for a in (lnx_g, lnx_b, wn, wr, wo, ng, nb)],
        out_specs=tok(D),
        out_shape=jax.ShapeDtypeStruct((B, S, D), F32),
        compiler_params=_params("arbitrary", "arbitrary"),
        name="mixer_merge",
    )(h3, y_nsa, y_hm, bonus, g, mg3, lnx_g, lnx_b, wn, wr, wo, ng, nb)


RT_TM = 256
NO_RANK = 99.0


def _route_body(x_ref, wq_ref, keys_ref, jq_ref, r2_ref, c1_ref, p2_ref):
    TM = RT_TM
    K = PEER_TOPK
    H = PEER_HEADS
    qt = _dot_nt(wq_ref[...], x_ref[...], HI)
    sc, rk, top = {}, {}, {}
    for h in range(H):
        for p in range(2):
            base = (2 * h + p) * PEER_HALF
            s = _dot(keys_ref[h, p], qt[base:base + PEER_HALF, :], HI)
            work = s
            rank = jnp.full(s.shape, NO_RANK, F32)
            vals = []
            for r in range(K):
                m = jnp.max(work, axis=0, keepdims=True)
                hit = work == m
                rank = jnp.where(hit, float(r + 1), rank)
                work = jnp.where(hit, -jnp.inf, work)
                vals.append(m)
            sc[h, p], rk[h, p], top[h, p] = s, rank, vals
    v1 = [jnp.concatenate([top[h, 0][i] for h in range(H)], axis=0) for i in range(K)]
    v2 = [jnp.concatenate([top[h, 1][i] for h in range(H)], axis=0) for i in range(K)]
    pairs = [(i, j) for i in range(K) for j in range(K) if (i + 1) * (j + 1) <= K]
    cand = {ij: v1[ij[0]] + v2[ij[1]] for ij in pairs}
    work = dict(cand)
    tau = None
    for _ in range(K):
        tau = functools.reduce(jnp.maximum, work.values())
        work = {ij: jnp.where(w == tau, -jnp.inf, w) for ij, w in work.items()}
    cmax = cand[0, 0]
    z = jnp.zeros_like(cmax)
    jcnt = [jnp.zeros_like(cmax) for _ in range(K)]
    for (i, j), cv in cand.items():
        keep = cv >= tau
        z = z + jnp.where(keep, jnp.exp(cv - cmax), 0.0)
        jcnt[i] = jcnt[i] + jnp.where(keep, 1.0, 0.0)
    zinv = 1.0 / z
    for h in range(H):
        r1 = rk[h, 0]
        jq = jnp.zeros(r1.shape, F32)
        for i in range(K):
            jq = jnp.where(r1 == float(i + 1), jcnt[i][h:h + 1, :], jq)
        jq_ref[h] = jq
        r2_ref[h] = rk[h, 1].astype(BF16)
        c1_ref[h] = jnp.exp(sc[h, 0] - top[h, 0][0]) * zinv[h:h + 1, :]
        p2_ref[h] = jnp.exp(sc[h, 1] - top[h, 1][0]).astype(BF16)


def _route(x1, wq_t, keys):
    T, D = x1.shape
    TM = RT_TM
    H = PEER_HEADS
    out = pl.BlockSpec((H, N_KEYS, TM), lambda i: (0, 0, i))
    shp = lambda dt: jax.ShapeDtypeStruct((H, N_KEYS, T), dt)
    return pl.pallas_call(
        _route_body,
        grid=(T // TM,),
        in_specs=[pl.BlockSpec((TM, D), lambda i: (i, 0)),
                  pl.BlockSpec(wq_t.shape, lambda i: (0, 0)),
                  pl.BlockSpec(keys.shape, lambda i: (0, 0, 0, 0))],
        out_specs=[out] * 4,
        out_shape=[shp(F32), shp(BF16), shp(F32), shp(BF16)],
        compiler_params=_params("arbitrary"),
        name="peer_route",
    )(x1, wq_t, keys)


PE_TM = 512
PE_EB = 512


def _peer_body(x_ref, u_ref, vt_ref, jq_ref, r2_ref, c1_ref, p2_ref, ng_ref, nb_ref, o_ref, acc):
    j = pl.program_id(1)
    TM = PE_TM
    groups = PE_EB // N_KEYS

    @pl.when(j == 0)
    def _():
        acc[...] = jnp.zeros_like(acc)

    x = x_ref[...]
    hid = _gelu(_dot_nt(u_ref[...], x.astype(BF16)))
    ws = []
    for aa in range(groups):
        a = j * groups + aa
        w = jnp.zeros((N_KEYS, TM), BF16)
        for h in range(PEER_HEADS):
            jq = jq_ref[h, pl.ds(a, 1), :].astype(BF16)
            c1 = c1_ref[h, pl.ds(a, 1), :].astype(BF16)
            w = w + jnp.where(r2_ref[h] <= jq, p2_ref[h], jnp.zeros((), BF16)) * c1
        ws.append(w)
    gh = (jnp.concatenate(ws, axis=0).astype(F32) * hid).astype(BF16)
    acc[...] += _dot(vt_ref[...], gh)

    @pl.when(j == pl.num_programs(1) - 1)
    def _():
        o_ref[...] = _layer_norm(ALPHA * x + acc[...].T, ng_ref[...], nb_ref[...])


def _peer(x1, u_b, vt_b, jq, r2, c1, p2, ng, nb):
    T, D = x1.shape
    TM, EB = PE_TM, PE_EB
    H = PEER_HEADS
    rt = pl.BlockSpec((H, N_KEYS, TM), lambda i, j: (0, 0, i))
    full = lambda a: pl.BlockSpec(a.shape, lambda i, j: (0,) * a.ndim)
    return pl.pallas_call(
        _peer_body,
        grid=(T // TM, N_EXPERTS // EB),
        in_specs=[pl.BlockSpec((TM, D), lambda i, j: (i, 0)),
                  pl.BlockSpec((EB, D), lambda i, j: (j, 0)),
                  pl.BlockSpec((D, EB), lambda i, j: (0, j)),
                  rt, rt, rt, rt, full(ng), full(nb)],
        out_specs=pl.BlockSpec((TM, D), lambda i, j: (i, 0)),
        out_shape=jax.ShapeDtypeStruct((T, D), F32),
        scratch_shapes=[pltpu.VMEM((D, TM), F32)],
        compiler_params=_params("arbitrary", "arbitrary"),
        name="peer_experts",
    )(x1, u_b, vt_b, jq, r2, c1, p2, ng, nb)


def _static_tables(S):
    nsb = S // SEL_BLOCK
    nc = S // CMP_STRIDE
    c0 = np.arange(nc) * CMP_STRIDE
    s0 = np.arange(nsb) * SEL_BLOCK
    ov = np.clip(np.minimum(c0[:, None] + CMP_BLOCK, s0[None, :] + SEL_BLOCK)
                 - np.maximum(c0[:, None], s0[None, :]), 0, None).astype(np.float32) / CMP_BLOCK
    ov[nc - 1:, :] = 0.0
    expand = np.zeros((128, S), np.float32)
    expand[np.arange(S) // SEL_BLOCK, np.arange(S)] = 1.0
    hid = np.arange(RWKV_DIM) // RWKV_HEAD_DIM
    bd = (hid[:, None] == hid[None, :]).astype(np.float32)
    t = np.arange(RW_TM)
    tri = ((t[:, None] >= t[None, :]) & (t[:, None] // RWKV_CHUNK == t[None, :] // RWKV_CHUNK)).astype(np.float32)
    return jnp.asarray(ov.T), jnp.asarray(expand, BF16), jnp.asarray(bd), jnp.asarray(tri)


def _nsa(q, kv, gate, rel_bias, cmp_pos, cmp_w1, cmp_b1, cmp_w2, cmp_b2, B, S, tables):
    G = NSA_GROUPS
    ovt, expand = tables
    kv5 = kv.reshape(B, S, 6, G, HEAD_DIM)
    chunks = lambda a: a.reshape(B, S // CMP_STRIDE, CMP_STRIDE, G, HEAD_DIM).transpose(0, 3, 1, 2, 4) \
        .reshape(B, G, S // CMP_STRIDE, CMP_STRIDE * HEAD_DIM)
    xc = jnp.stack([chunks(kv5[:, :, 0]), chunks(kv5[:, :, 1])])
    kvc = _compress(xc, cmp_pos.reshape(2, 1, CMP_BLOCK * HEAD_DIM), cmp_w1, cmp_b1.reshape(2, 1, CMP_HIDDEN),
                    cmp_w2, cmp_b2.reshape(2, 1, HEAD_DIM))
    bias_cmp, gsel, gwin = _bias_tables(rel_bias, S)
    q3 = q.reshape(B, S, NSA_DIM)
    o_cmp, sel = _cmp_attn(q3, kvc, bias_cmp, ovt)
    heads_first = lambda a: a.transpose(0, 2, 1, 3).astype(BF16)
    ks, vs = heads_first(kv5[:, :, 2]), heads_first(kv5[:, :, 3])
    padw = lambda a: jnp.pad(heads_first(a), ((0, 0), (0, 0), (WINDOW, 0), (0, 0)))
    kw, vw = padw(kv5[:, :, 4]), padw(kv5[:, :, 5])
    gate_g = gate.reshape(B, S, GATE_PAD)[:, :, :3 * NSA_HEADS].reshape(B, S, G, 3 * NSA_HPG).transpose(0, 2, 1, 3)
    gate_g = jnp.pad(gate_g, ((0, 0), (0, 0), (0, 0), (0, 128 - 3 * NSA_HPG)))
    return _nsa_main(q3, ks, vs, kw, vw, sel, expand, gsel, gwin, gate_g, o_cmp)


def kernel(x, ln_in_g, ln_in_b, rel_bias, w_in, token_mu, cmp_pos, cmp_w1, cmp_b1, cmp_w2, cmp_b2, rwkv_w0, rwkv_w2, rwkv_a0, rwkv_a2, rwkv_g2, rwkv_k_k, rwkv_k_a, rwkv_r_k, rwkv_lnx_g, rwkv_lnx_b, w_o_nsa, w_o_rwkv, w_out, ln_mix_g, ln_mix_b, peer_w_query, peer_sub_keys, peer_u, peer_v, ln_ffn_g, ln_ffn_b):
    B, S, D = x.shape
    T = B * S
    row = lambda a: a.reshape(1, -1)
    ovt, expand, bd, tri = _static_tables(S)

    segs = jnp.split(w_in[0], IN_SPLITS, axis=1)
    w_pad = jnp.concatenate(segs[:7] + [jnp.pad(segs[7], ((0, 0), (0, GATE_PAD - 3 * NSA_HEADS)))] + segs[8:],
                            axis=1).astype(BF16)
    h, q, kv, gate, rw, mg = _inproj(x.reshape(T, D), row(ln_in_g), row(ln_in_b), w_pad)

    y_nsa = _nsa(q, kv, gate, rel_bias, cmp_pos[0], cmp_w1[0], cmp_b1[0], cmp_w2[0], cmp_b2[0], B, S, (ovt, expand))

    zpad = jnp.zeros((LORA_W, RWKV_DIM), F32)
    w2p = jnp.concatenate([rwkv_w2[0], zpad], axis=0)
    a2p = jnp.concatenate([zpad, rwkv_a2[0]], axis=0)
    ab, rb, bt, kt, v, pc, g, bonus = _rwkv_prep(
        rw.reshape(B, S, RWKV_COLS), row(token_mu[0]), row(rwkv_w0[0]), w2p, row(rwkv_a0[0]), a2p, rwkv_g2[0],
        row(rwkv_k_k[0]), row(rwkv_k_a[0]), row(rwkv_r_k[0]), bd, tri)
    y_hm = _rwkv_scan(ab, rb, bt, kt, v, pc.reshape(B, RWKV_HEADS, S // RWKV_CHUNK, 1, RWKV_HEAD_DIM))

    x1 = _mix(h.reshape(B, S, D), y_nsa, y_hm, bonus, g, mg.reshape(B, S, 2 * D), row(rwkv_lnx_g[0]),
              row(rwkv_lnx_b[0]), w_o_nsa[0].astype(BF16), w_o_rwkv[0].astype(BF16), w_out[0].astype(BF16),
              row(ln_mix_g[0]), row(ln_mix_b[0])).reshape(T, D)

    jq, r2, c1, p2 = _route(x1, peer_w_query[0].T, peer_sub_keys[0])
    out = _peer(x1, peer_u[0].astype(BF16), peer_v[0].T.astype(BF16), jq, r2, c1, p2,
                row(ln_ffn_g[0]), row(ln_ffn_b[0]))
    return out.reshape(B, S, D)
```

```python
import functools
import math

import numpy as np
import jax
import jax.numpy as jnp
from jax import lax
from jax.experimental import pallas as pl
from jax.experimental.pallas import tpu as pltpu

F32 = jnp.float32
BF16 = jnp.bfloat16
HI = lax.Precision.HIGHEST

D_MODEL = 1024
NSA_HEADS = 8
NSA_GROUPS = 2
NSA_HPG = NSA_HEADS // NSA_GROUPS
HEAD_DIM = 64
NSA_DIM = NSA_HEADS * HEAD_DIM
NSA_KV_DIM = NSA_GROUPS * HEAD_DIM
CMP_BLOCK = 32
CMP_STRIDE = 16
CMP_HIDDEN = 256
SEL_BLOCK = 64
SEL_TOPN = 16
WINDOW = 512
Q_BLOCK = 128
FORCE_BONUS = 1.0e4
ATTN_SCALE = HEAD_DIM ** -0.5
RWKV_HEADS = 8
RWKV_HEAD_DIM = 64
RWKV_DIM = RWKV_HEADS * RWKV_HEAD_DIM
LORA_W = 64
LORA_A = 64
LORA_G = 128
RWKV_GN_EPS = 64e-5
N_BUCKETS = 32
MAX_DISTANCE = 128
PEER_HEADS = 8
N_KEYS = 128
N_EXPERTS = N_KEYS * N_KEYS
PEER_HALF = 64
PEER_TOPK = 16
DEPTH = 1
ALPHA = (2.0 * DEPTH) ** 0.25
LN_EPS = 1e-5
NEG = -1e30
RWKV_COLS = 3 * RWKV_DIM + LORA_W + LORA_A + LORA_G
IN_SIZES = (NSA_DIM,) + (NSA_KV_DIM,) * 6 + (3 * NSA_HEADS, RWKV_COLS, 2 * D_MODEL)
IN_SPLITS = tuple(int(c) for c in np.cumsum(IN_SIZES)[:-1])

GATE_PAD = 128
Z_PAD = NSA_DIM + 6 * NSA_KV_DIM + GATE_PAD + RWKV_COLS + 2 * D_MODEL
RWKV_CHUNK = 64
VMEM_LIMIT = 56 * 1024 * 1024


def _params(*sem):
    return pltpu.CompilerParams(dimension_semantics=sem, vmem_limit_bytes=VMEM_LIMIT)


def _dot(a, b, prec=None):
    return jnp.dot(a, b, preferred_element_type=F32, precision=prec)


def _dot_nt(a, b, prec=None):
    return lax.dot_general(a, b, (((1,), (1,)), ((), ())), preferred_element_type=F32, precision=prec)


def _dot_tn(a, b, prec=None):
    return lax.dot_general(a, b, (((0,), (0,)), ((), ())), preferred_element_type=F32, precision=prec)


def _layer_norm(x, g, b):
    mu = jnp.mean(x, axis=-1, keepdims=True)
    xc = x - mu
    var = jnp.mean(xc * xc, axis=-1, keepdims=True)
    return xc * lax.rsqrt(var + LN_EPS) * g + b


_GELU_C = math.sqrt(2.0 / math.pi)


def _gelu(x):
    return 0.5 * x * (1.0 + jnp.tanh(_GELU_C * (x + 0.044715 * (x * x * x))))


def _sigmoid(x):
    return 1.0 / (1.0 + jnp.exp(-x))


def _iota(shape, axis):
    return lax.broadcasted_iota(jnp.int32, shape, axis)


def _bucket_table(n):
    d = np.arange(n)
    large = 16 + (np.log(np.maximum(d, 1).astype(np.float64) / 16.0) / math.log(MAX_DISTANCE / 16.0) * 16.0).astype(np.int64)
    large = np.minimum(large, N_BUCKETS - 1)
    return np.where(d < 16, d, large).astype(np.int32)


def _bucket_thresholds():
    bt = _bucket_table(4096)
    thr = [int(np.argmax(bt >= b)) for b in range(17, N_BUCKETS)]
    rebuilt = np.where(np.arange(4096) < 16, np.arange(4096), 16 + sum((np.arange(4096) >= t).astype(np.int64) for t in thr))
    assert (rebuilt == bt).all()
    return thr


def _bias_of_distance(rb_ref, dist, h, delta):
    d = jnp.maximum(dist, 0)
    b = jnp.where(d < 16, d, 16)
    for thr in _bucket_thresholds():
        b = b + jnp.where(d >= thr, 1, 0)
    far = rb_ref[N_BUCKETS - 1, h]
    val = jnp.full(d.shape, far, F32)
    for k in range(N_BUCKETS - 1):
        val = jnp.where(b == k, rb_ref[k, h], val)
    return val - far if delta else val


CMP_BT = 256


def _bias_cmp_body(rb_ref, o_ref):
    c = _iota(o_ref.shape[1:], 0)
    t = pl.program_id(0) * CMP_BT + _iota(o_ref.shape[1:], 1)
    dist = t - (c * CMP_STRIDE + CMP_BLOCK - 1)
    for h in range(NSA_HEADS):
        o_ref[h] = _bias_of_distance(rb_ref, dist, h, False)


def _bias_near_body(rb_ref, dsel_ref, dwin_ref):
    for ref, lead in ((dsel_ref, Q_BLOCK), (dwin_ref, WINDOW)):
        nk = ref.shape[1]
        dist = _iota((nk, Q_BLOCK), 1) + lead - _iota((nk, Q_BLOCK), 0)
        for g in range(NSA_GROUPS):
            ref[g] = jnp.concatenate([_bias_of_distance(rb_ref, dist, g * NSA_HPG + h, True)
                                      for h in range(NSA_HPG)], axis=1)


def _bias_tables(rel_bias, S):
    nc = S // CMP_STRIDE
    smem = pl.BlockSpec(memory_space=pltpu.SMEM)
    bias_cmp = pl.pallas_call(
        _bias_cmp_body,
        grid=(S // CMP_BT,),
        in_specs=[smem],
        out_specs=pl.BlockSpec((NSA_HEADS, nc, CMP_BT), lambda i: (0, 0, i)),
        out_shape=jax.ShapeDtypeStruct((NSA_HEADS, nc, S), F32),
        compiler_params=_params("arbitrary"),
        name="nsa_bias_cmp",
    )(rel_bias)
    dsel, dwin = pl.pallas_call(
        _bias_near_body,
        in_specs=[smem],
        out_shape=[jax.ShapeDtypeStruct((NSA_GROUPS, 2 * Q_BLOCK, NSA_HPG * Q_BLOCK), F32),
                   jax.ShapeDtypeStruct((NSA_GROUPS, WINDOW + Q_BLOCK, NSA_HPG * Q_BLOCK), F32)],
        name="nsa_bias_near",
    )(rel_bias)
    return bias_cmp, dsel, dwin


_Z_SEGS = (("kv", 0, 768), ("gate", 768, 896), ("rw", 896, 2688), ("mg", 2688, 4736))


def _inproj_body(x_ref, g_ref, b_ref, wq_ref, w_ref, h_ref, qt_ref, kv_ref, gate_ref, rw_ref, mg_ref):
    h = _layer_norm(x_ref[...], g_ref[...], b_ref[...])
    h_ref[...] = h
    hb = h.astype(BF16)
    qt_ref[...] = _dot_nt(wq_ref[...], hb)
    for ref, (_, lo, hi) in zip((kv_ref, gate_ref, rw_ref, mg_ref), _Z_SEGS):
        ref[...] = _dot(hb, w_ref[:, lo:hi])


def _inproj(x2, ln_g, ln_b, wq_t, w_pad, B, S):
    T = x2.shape[0]
    TM = 256
    per = S // TM
    row = lambda n: pl.BlockSpec((TM, n), lambda i: (i, 0))
    full = lambda a: pl.BlockSpec(a.shape, lambda i: (0,) * a.ndim)
    widths = [D_MODEL] + [hi - lo for _, lo, hi in _Z_SEGS]
    tok = [jax.ShapeDtypeStruct((T, n), F32) for n in widths]
    return pl.pallas_call(
        _inproj_body,
        grid=(T // TM,),
        in_specs=[row(D_MODEL), full(ln_g), full(ln_b), full(wq_t), full(w_pad)],
        out_specs=[row(D_MODEL), pl.BlockSpec((None, NSA_DIM, TM), lambda i: (i // per, 0, i % per))]
        + [row(n) for n in widths[1:]],
        out_shape=[tok[0], jax.ShapeDtypeStruct((B, NSA_DIM, S), F32)] + tok[1:],
        compiler_params=_params("arbitrary"),
        name="ln_inproj",
    )(x2, ln_g, ln_b, wq_t, w_pad)


def _compress_body(x_ref, pos_ref, w1_ref, b1_ref, w2_ref, b2_ref, o_ref):
    x = x_ref[...]
    pos = pos_ref[...]
    half = CMP_STRIDE * HEAD_DIM
    a = _dot((x + pos[:, :half]).astype(BF16), w1_ref[:half, :].astype(BF16))
    b = _dot((x + pos[:, half:]).astype(BF16), w1_ref[half:, :].astype(BF16))
    nxt = pltpu.roll(b, x.shape[0] - 1, 0)
    hid = _gelu(a + nxt + b1_ref[...])
    o_ref[...] = _dot(hid.astype(BF16), w2_ref[...].astype(BF16)) + b2_ref[...]


def _compress(xc, pos, w1, b1, w2, b2):
    _, B, G, NCH, W = xc.shape
    sq = lambda *shape: pl.BlockSpec((None,) + shape, lambda i, b, g: (i,) + (0,) * len(shape))
    return pl.pallas_call(
        _compress_body,
        grid=(2, B, G),
        in_specs=[pl.BlockSpec((None, None, None, NCH, W), lambda i, b, g: (i, b, g, 0, 0)),
                  sq(1, 2 * W), sq(2 * W, CMP_HIDDEN), sq(1, CMP_HIDDEN), sq(CMP_HIDDEN, HEAD_DIM), sq(1, HEAD_DIM)],
        out_specs=pl.BlockSpec((None, None, None, NCH, HEAD_DIM), lambda i, b, g: (i, b, g, 0, 0)),
        out_shape=jax.ShapeDtypeStruct((2, B, G, NCH, HEAD_DIM), F32),
        compiler_params=_params("arbitrary", "arbitrary", "arbitrary"),
        name="nsa_compress",
    )(xc, pos, w1, b1, w2, b2)


CMP_TQ = 256


def _cmp_attn_body(qt_ref, kc_ref, vc_ref, bias_ref, ovt_ref, o_ref, sel_ref):
    TQ = CMP_TQ
    t0 = pl.program_id(2) * TQ
    kc = kc_ref[...].astype(BF16)
    vc = vc_ref[...].astype(BF16)
    nc = kc.shape[0]
    c = _iota((nc, TQ), 0)
    t = t0 + _iota((nc, TQ), 1)
    mask = (t - (c * CMP_STRIDE + CMP_BLOCK - 1)) >= 0
    psum = jnp.zeros((nc, TQ), F32)
    for h in range(NSA_HPG):
        hs = slice(h * HEAD_DIM, (h + 1) * HEAD_DIM)
        lg = jnp.where(mask, _dot(kc, qt_ref[hs, :].astype(BF16)) + bias_ref[h], NEG)
        m = jnp.max(lg, axis=0, keepdims=True)
        e = jnp.exp(lg - m)
        p = jnp.where(mask, e / jnp.sum(e, axis=0, keepdims=True), 0.0)
        psum = psum + p
        o_ref[hs, :] = _dot_tn(vc, p.astype(BF16))

    nsb = ovt_ref.shape[0]
    imp = _dot(ovt_ref[...], psum, HI)
    j = _iota((nsb, TQ), 0)
    cur = (t0 + _iota((nsb, TQ), 1)) // SEL_BLOCK
    forced = (j == 0) | (j == cur) | (j == cur - 1)
    score = jnp.where(j > cur, NEG, imp + jnp.where(forced, FORCE_BONUS, 0.0))
    rank = jnp.zeros((nsb, TQ), F32)
    for i in range(nsb):
        si = score[i:i + 1, :]
        better = (si > score) | ((si == score) & (j > i))
        rank = rank + jnp.where(better, 1.0, 0.0)
    sel = jnp.where((rank < SEL_TOPN) & (j <= cur), 1.0, 0.0)
    sel_ref[...] = jnp.concatenate([sel, jnp.zeros((128 - nsb, TQ), F32)], axis=0)


def _cmp_attn(qt, kvc, bias_cmp, ovt):
    B, _, S = qt.shape
    G = NSA_GROUPS
    GW = NSA_HPG * HEAD_DIM
    NCH = kvc.shape[3]
    return pl.pallas_call(
        _cmp_attn_body,
        grid=(B, G, S // CMP_TQ),
        in_specs=[pl.BlockSpec((None, GW, CMP_TQ), lambda b, g, i: (b, g, i)),
                  pl.BlockSpec((None, None, None, NCH, HEAD_DIM), lambda b, g, i: (0, b, g, 0, 0)),
                  pl.BlockSpec((None, None, None, NCH, HEAD_DIM), lambda b, g, i: (1, b, g, 0, 0)),
                  pl.BlockSpec((NSA_HPG, NCH, CMP_TQ), lambda b, g, i: (g, 0, i)),
                  pl.BlockSpec(ovt.shape, lambda b, g, i: (0, 0))],
        out_specs=[pl.BlockSpec((None, GW, CMP_TQ), lambda b, g, i: (b, g, i)),
                   pl.BlockSpec((None, None, 128, CMP_TQ), lambda b, g, i: (b, g, 0, i))],
        out_shape=[jax.ShapeDtypeStruct((B, NSA_DIM, S), F32),
                   jax.ShapeDtypeStruct((B, G, 128, S), F32)],
        compiler_params=_params("arbitrary", "arbitrary", "arbitrary"),
        name="nsa_cmp_select",
    )(qt, kvc, kvc, bias_cmp, ovt)


SEL_FAR = 512
SEL_PAD = 512


def _nsa_main_body(qt_ref, ks_ref, vs_ref, kw_ref, vw_ref, sel_ref, et_ref, dsel_ref, dwin_ref,
                   gate_ref, oc_ref, o_ref):
    TQ = Q_BLOCK
    W4 = NSA_HPG * TQ
    qi = pl.program_id(2)
    t0 = pl.multiple_of(qi * TQ, TQ)
    qt = qt_ref[...]
    q4 = jnp.concatenate([qt[h * HEAD_DIM:(h + 1) * HEAD_DIM, :] for h in range(NSA_HPG)], axis=1).astype(BF16)
    selb = sel_ref[...].astype(BF16)
    rep = lambda m: jnp.concatenate([m] * NSA_HPG, axis=1)

    def online(lg, mask, v, carry):
        m, l, acc = carry
        lg = jnp.where(mask, lg, NEG)
        m_new = jnp.maximum(m, jnp.max(lg, axis=0, keepdims=True))
        alpha = jnp.exp(m - m_new)
        p = jnp.where(mask, jnp.exp(lg - m_new), 0.0)
        return (m_new, alpha * l + jnp.sum(p, axis=0, keepdims=True),
                alpha * acc + _dot_tn(v, p.astype(BF16)))

    near0 = pl.multiple_of(t0 + SEL_PAD - TQ, TQ)

    def far_chunk(c, carry):
        r0 = pl.multiple_of(near0 - (c + 1) * SEL_FAR, TQ)
        hit = _dot(et_ref[pl.ds(r0, SEL_FAR), :], selb) > 0.5
        return online(_dot(ks_ref[pl.ds(r0, SEL_FAR), :], q4), rep(hit), vs_ref[pl.ds(r0, SEL_FAR), :], carry)

    carry = (jnp.full((1, W4), NEG, F32), jnp.zeros((1, W4), F32), jnp.zeros((HEAD_DIM, W4), F32))
    n_far = (jnp.maximum(qi - 1, 0) * TQ + SEL_FAR - 1) // SEL_FAR
    carry = lax.fori_loop(0, n_far, far_chunk, carry)
    j = _iota((2 * TQ, TQ), 0)
    i = _iota((2 * TQ, TQ), 1)
    hit = (_dot(et_ref[pl.ds(near0, 2 * TQ), :], selb) > 0.5) & (j <= i + TQ)
    _, l, acc = online(_dot(ks_ref[pl.ds(near0, 2 * TQ), :], q4) + dsel_ref[...], rep(hit),
                       vs_ref[pl.ds(near0, 2 * TQ), :], carry)
    o_slc = acc / l

    span = WINDOW + TQ
    jw = _iota((span, TQ), 0)
    dist = _iota((span, TQ), 1) + WINDOW - jw
    m_win = rep((dist >= 0) & (dist < WINDOW) & (t0 - WINDOW + jw >= 0))
    lg = jnp.where(m_win, _dot(kw_ref[pl.ds(t0, span), :], q4) + dwin_ref[...], NEG)
    p = jnp.where(m_win, jnp.exp(lg - jnp.max(lg, axis=0, keepdims=True)), 0.0)
    o_win = _dot_tn(vw_ref[pl.ds(t0, span), :], p.astype(BF16)) / jnp.sum(p, axis=0, keepdims=True)

    gs = _sigmoid(gate_ref[...])
    for h in range(NSA_HPG):
        hs = slice(h * HEAD_DIM, (h + 1) * HEAD_DIM)
        ts = slice(h * TQ, (h + 1) * TQ)
        g0, g1, g2 = (gs[3 * h + r:3 * h + r + 1, :] for r in range(3))
        o_ref[hs, :] = g0 * oc_ref[hs, :] + g1 * o_slc[:, ts] + g2 * o_win[:, ts]


def _nsa_main(qt, ks, vs, kw, vw, sel, et, dsel, dwin, gate_g, o_cmp):
    B, _, S = qt.shape
    G = NSA_GROUPS
    GW = NSA_HPG * HEAD_DIM
    TQ = Q_BLOCK
    qspec = pl.BlockSpec((None, GW, TQ), lambda g, b, i: (b, g, i))
    kvspec = lambda a: pl.BlockSpec((None, None) + a.shape[2:], lambda g, b, i: (b, g, 0, 0))
    tspec = lambda a: pl.BlockSpec((None,) + a.shape[1:], lambda g, b, i: (g, 0, 0))
    tile = lambda n: pl.BlockSpec((None, None, n, TQ), lambda g, b, i: (b, g, 0, i))
    return pl.pallas_call(
        _nsa_main_body,
        grid=(G, B, S // TQ),
        in_specs=[qspec, kvspec(ks), kvspec(vs), kvspec(kw), kvspec(vw), tile(sel.shape[2]),
                  pl.BlockSpec(et.shape, lambda g, b, i: (0, 0)), tspec(dsel), tspec(dwin),
                  tile(gate_g.shape[2]), qspec],
        out_specs=qspec,
        out_shape=jax.ShapeDtypeStruct((B, NSA_DIM, S), F32),
        compiler_params=_params("arbitrary", "arbitrary", "arbitrary"),
        name="nsa_select_window",
    )(qt, ks, vs, kw, vw, sel, et, dsel, dwin, gate_g, o_cmp)


RW_TM = 512


def _rwkv_prep_body(rw_ref, mu_ref, w0_ref, w2_ref, a0_ref, a2_ref, g2_ref, kk_ref, ka_ref, rk_ref,
                    bd_ref, tri_ref, ab_ref, rb_ref, bt_ref, kt_ref, v_ref, pc_ref, g_ref, bonus_ref, carry):
    TM = RW_TM
    D = RWKV_DIM

    @pl.when(pl.program_id(1) == 0)
    def _():
        carry[...] = jnp.zeros_like(carry)

    x = rw_ref[...]
    last = carry[0:1, :]
    prev = jnp.where(_iota(x.shape, 0) == 0, last, pltpu.roll(x, 1, 0))
    carry[0:1, :] = x[TM - 1:TM, :]
    xm = x + mu_ref[...] * (prev - x)
    r, k, v = xm[:, 0:D], xm[:, D:2 * D], xm[:, 2 * D:3 * D]
    lo = xm[:, 3 * D:3 * D + LORA_W + LORA_A]
    g_lo = xm[:, 3 * D + LORA_W + LORA_A:]
    y = -(w0_ref[...] + _dot(jnp.tanh(lo), w2_ref[...], HI))
    softplus = jnp.maximum(y, 0.0) + jnp.log(1.0 + jnp.exp(-jnp.abs(y)))
    logd = -jnp.exp(-softplus - 0.5)
    a = _sigmoid(a0_ref[...] + _dot(lo, a2_ref[...], HI))
    g_ref[...] = _dot(_sigmoid(g_lo), g2_ref[...], HI)
    kk = k * kk_ref[...]
    kk = kk / jnp.maximum(jnp.sqrt(_dot(kk * kk, bd_ref[...], HI)), 1e-12)
    k2 = k * (1.0 + (a - 1.0) * ka_ref[...])
    bonus_ref[...] = _dot(r * k2 * rk_ref[...], bd_ref[...], HI) * v

    cum = _dot(tri_ref[...], logd, HI)
    p_in = jnp.exp(cum)
    p_inv = jnp.exp(-cum)
    ab = -kk * jnp.exp(cum - logd)
    rb = r * p_in
    bt = kk * a * p_inv
    kt = k2 * p_inv
    C = RWKV_CHUNK
    pc = jnp.concatenate([p_in[c * C + C - 1:c * C + C, :] for c in range(TM // C)], axis=0)
    for h in range(RWKV_HEADS):
        sl = slice(h * RWKV_HEAD_DIM, (h + 1) * RWKV_HEAD_DIM)
        ab_ref[h] = ab[:, sl]
        rb_ref[h] = rb[:, sl]
        bt_ref[h] = bt[:, sl]
        kt_ref[h] = kt[:, sl]
        v_ref[h] = v[:, sl]
        pc_ref[h] = pc[:, sl]


def _rwkv_prep(rw3, mu, w0, w2p, a0, a2p, g2, k_k, k_a, r_k, bd, tri):
    B, S, W = rw3.shape
    TM = RW_TM
    H, N = RWKV_HEADS, RWKV_HEAD_DIM
    full = lambda a: pl.BlockSpec(a.shape, lambda b, i: (0,) * a.ndim)
    hm = pl.BlockSpec((None, H, TM, N), lambda b, i: (b, 0, i, 0))
    tok = pl.BlockSpec((None, TM, RWKV_DIM), lambda b, i: (b, i, 0))
    hm_shape = jax.ShapeDtypeStruct((B, H, S, N), F32)
    return pl.pallas_call(
        _rwkv_prep_body,
        grid=(B, S // TM),
        in_specs=[pl.BlockSpec((None, TM, W), lambda b, i: (b, i, 0))]
        + [full(a) for a in (mu, w0, w2p, a0, a2p, g2, k_k, k_a, r_k, bd, tri)],
        out_specs=[hm] * 5 + [pl.BlockSpec((None, H, TM // RWKV_CHUNK, N), lambda b, i: (b, 0, i, 0)), tok, tok],
        out_shape=[hm_shape] * 5 + [jax.ShapeDtypeStruct((B, H, S // RWKV_CHUNK, N), F32),
                                    jax.ShapeDtypeStruct((B, S, RWKV_DIM), F32),
                                    jax.ShapeDtypeStruct((B, S, RWKV_DIM), F32)],
        scratch_shapes=[pltpu.VMEM((8, W), F32)],
        compiler_params=_params("arbitrary", "arbitrary"),
        name="rwkv_prep",
    )(rw3, mu, w0, w2p, a0, a2p, g2, k_k, k_a, r_k, bd, tri)


def _rwkv_scan_body(ab_ref, rb_ref, bt_ref, kt_ref, v_ref, pc_ref, y_ref, s_ref):
    C = RWKV_CHUNK
    bf = lambda a: a.astype(BF16)

    @pl.when(pl.program_id(1) == 0)
    def _():
        s_ref[...] = jnp.zeros_like(s_ref)

    row = _iota((C, 2 * C), 0)
    col = _iota((C, 2 * C), 1)
    colm = jnp.where(col >= C, col - C, col)
    strict = row > colm
    incl = row >= colm
    left = col < C
    eye_right = jnp.where(col - C == row, 1.0, 0.0)
    zeros = jnp.zeros((C, RWKV_HEAD_DIM), F32)
    H = range(RWKV_HEADS)
    v = [v_ref[h] for h in H]
    s0 = [s_ref[h] for h in H]
    ar = [bf(jnp.concatenate([ab_ref[h], rb_ref[h]], axis=0)) for h in H]
    bk = [bf(jnp.concatenate([bt_ref[h], kt_ref[h]], axis=0)) for h in H]
    m = [_dot_nt(ar[h], bk[h]) for h in H]
    ars = [_dot_nt(ar[h], bf(s0[h])) for h in H]
    lcat = [jnp.where(strict, m[h][:C], 0.0) for h in H]
    mcat = [bf(jnp.where(incl, m[h][C:], 0.0)) for h in H]
    lv = [_dot(bf(lcat[h]), bf(jnp.concatenate([zeros, v[h]], axis=0))) for h in H]
    pi = [jnp.where(left, lcat[h], eye_right) for h in H]
    for _ in range(int(math.log2(C))):
        r = [_dot(bf(pi[h][:, :C]), bf(pi[h])) for h in H]
        pi = [jnp.where(left, r[h], pi[h] + r[h]) for h in H]
    u = [_dot(bf(pi[h]), bf(jnp.concatenate([zeros, ars[h][:C] + lv[h]], axis=0))) for h in H]
    uv = [bf(jnp.concatenate([u[h], v[h]], axis=0)) for h in H]
    ys = [ars[h][C:] + _dot(mcat[h], uv[h]) for h in H]
    sn = [(s0[h] + _dot_tn(uv[h], bk[h])) * pc_ref[h] for h in H]
    for h in H:
        y_ref[h] = ys[h]
        s_ref[h] = sn[h]


def _rwkv_scan(ab, rb, bt, kt, v, pc5):
    B, H, S, N = ab.shape
    C = RWKV_CHUNK
    blk = pl.BlockSpec((None, H, C, N), lambda b, c: (b, 0, c, 0))
    return pl.pallas_call(
        _rwkv_scan_body,
        grid=(B, S // C),
        in_specs=[blk] * 5 + [pl.BlockSpec((None, H, None, 1, N), lambda b, c: (b, 0, c, 0, 0))],
        out_specs=blk,
        out_shape=jax.ShapeDtypeStruct((B, H, S, N), F32),
        scratch_shapes=[pltpu.VMEM((H, N, N), F32)],
        compiler_params=_params("arbitrary", "arbitrary"),
        name="rwkv_scan",
    )(ab, rb, bt, kt, v, pc5)


MIX_TM = 256


def _mix_body(h_ref, yn_ref, y_ref, bonus_ref, g_ref, mg_ref, lg_ref, lb_ref, wn_ref, wr_ref, wo_ref,
              ng_ref, nb_ref, o_ref):
    ys = []
    for h in range(RWKV_HEADS):
        y = y_ref[h]
        mu = jnp.mean(y, axis=-1, keepdims=True)
        yc = y - mu
        var = jnp.mean(yc * yc, axis=-1, keepdims=True)
        ys.append(yc * lax.rsqrt(var + RWKV_GN_EPS))
    y = jnp.concatenate(ys, axis=1) * lg_ref[...] + lb_ref[...]
    y_rwkv = (y + bonus_ref[...]) * g_ref[...]
    gate = _sigmoid(mg_ref[...])
    a = _dot_tn(yn_ref[...].astype(BF16), wn_ref[...])
    b = _dot(y_rwkv.astype(BF16), wr_ref[...])
    m = gate[:, :D_MODEL] * a + gate[:, D_MODEL:] * b
    mixed = _dot(m.astype(BF16), wo_ref[...])
    o_ref[...] = _layer_norm(ALPHA * h_ref[...] + mixed, ng_ref[...], nb_ref[...])


def _mix(h3, y_nsa, y_hm, bonus, g, mg3, lnx_g, lnx_b, wn, wr, wo, ng, nb):
    B, S, D = h3.shape
    TM = MIX_TM
    tok = lambda n: pl.BlockSpec((None, TM, n), lambda b, i: (b, i, 0))
    full = lambda a: pl.BlockSpec(a.shape, lambda b, i: (0,) * a.ndim)
    return pl.pallas_call(
        _mix_body,
        grid=(B, S // TM),
        in_specs=[tok(D), pl.BlockSpec((None, NSA_DIM, TM), lambda b, i: (b, 0, i)),
                  pl.BlockSpec((None, RWKV_HEADS, TM, RWKV_HEAD_DIM), lambda b, i: (b, 0, i, 0)),
                  tok(RWKV_DIM), tok(RWKV_DIM), tok(2 * D)]
        + [full(a) for a in (lnx_g, lnx_b, wn, wr, wo, ng, nb)],
        out_specs=tok(D),
        out_shape=jax.ShapeDtypeStruct((B, S, D), F32),
        compiler_params=_params("arbitrary", "arbitrary"),
        name="mixer_merge",
    )(h3, y_nsa, y_hm, bonus, g, mg3, lnx_g, lnx_b, wn, wr, wo, ng, nb)


RT_TM = 256
NO_RANK = 99.0


def _route_body(x_ref, wq_ref, keys_ref, jq_ref, r2_ref, c1_ref, p2_ref):
    TM = RT_TM
    K = PEER_TOPK
    H = PEER_HEADS
    qt = _dot_nt(wq_ref[...], x_ref[...], HI)
    sc, rk, top = {}, {}, {}
    for h in range(H):
        for p in range(2):
            base = (2 * h + p) * PEER_HALF
            s = _dot(keys_ref[h, p], qt[base:base + PEER_HALF, :], HI)
            work = s
            rank = jnp.full(s.shape, NO_RANK, F32)
            vals = []
            for r in range(K):
                m = jnp.max(work, axis=0, keepdims=True)
                hit = work == m
                rank = jnp.where(hit, float(r + 1), rank)
                work = jnp.where(hit, -jnp.inf, work)
                vals.append(m)
            sc[h, p], rk[h, p], top[h, p] = s, rank, vals
    v1 = [jnp.concatenate([top[h, 0][i] for h in range(H)], axis=0) for i in range(K)]
    v2 = [jnp.concatenate([top[h, 1][i] for h in range(H)], axis=0) for i in range(K)]
    pairs = [(i, j) for i in range(K) for j in range(K) if (i + 1) * (j + 1) <= K]
    cand = {ij: v1[ij[0]] + v2[ij[1]] for ij in pairs}
    work = dict(cand)
    tau = None
    for _ in range(K):
        tau = functools.reduce(jnp.maximum, work.values())
        work = {ij: jnp.where(w == tau, -jnp.inf, w) for ij, w in work.items()}
    cmax = cand[0, 0]
    z = jnp.zeros_like(cmax)
    jcnt = [jnp.zeros_like(cmax) for _ in range(K)]
    for (i, j), cv in cand.items():
        keep = cv >= tau
        z = z + jnp.where(keep, jnp.exp(cv - cmax), 0.0)
        jcnt[i] = jcnt[i] + jnp.where(keep, 1.0, 0.0)
    zinv = 0.5 / z
    for h in range(H):
        r1 = rk[h, 0]
        jq = jnp.zeros(r1.shape, F32)
        for i in range(K):
            jq = jnp.where(r1 == float(i + 1), jcnt[i][h:h + 1, :], jq)
        jq_ref[h] = jq
        r2_ref[h] = rk[h, 1].astype(BF16)
        c1_ref[h] = jnp.exp(sc[h, 0] - top[h, 0][0]) * zinv[h:h + 1, :]
        p2_ref[h] = jnp.exp(sc[h, 1] - top[h, 1][0]).astype(BF16)


def _route(x1, wq_t, keys):
    T, D = x1.shape
    TM = RT_TM
    H = PEER_HEADS
    out = pl.BlockSpec((H, N_KEYS, TM), lambda i: (0, 0, i))
    shp = lambda dt: jax.ShapeDtypeStruct((H, N_KEYS, T), dt)
    return pl.pallas_call(
        _route_body,
        grid=(T // TM,),
        in_specs=[pl.BlockSpec((TM, D), lambda i: (i, 0)),
                  pl.BlockSpec(wq_t.shape, lambda i: (0, 0)),
                  pl.BlockSpec(keys.shape, lambda i: (0, 0, 0, 0))],
        out_specs=[out] * 4,
        out_shape=[shp(F32), shp(BF16), shp(F32), shp(BF16)],
        compiler_params=_params("arbitrary"),
        name="peer_route",
    )(x1, wq_t, keys)


PE_TM = 512
PE_EB = 512
PE_NB = N_EXPERTS // PE_EB


PE_WT = 256


def _peer_gate_gelu(block, tw, hid_ref, gh_ref, jq_ref, r2_ref, c1_ref, p2_ref):
    groups = PE_EB // N_KEYS
    LT = 128
    ws = slice(tw * PE_WT, (tw + 1) * PE_WT)
    for aa in range(groups):
        a = jnp.clip(block * groups + aa, 0, N_KEYS - 1)
        rs = slice(aa * N_KEYS, (aa + 1) * N_KEYS)
        w = jnp.zeros((N_KEYS, PE_WT), BF16)
        for h in range(PEER_HEADS):
            jq = jq_ref[h, pl.ds(a, 1), :][:, ws].astype(BF16)
            c1 = c1_ref[h, pl.ds(a, 1), :][:, ws].astype(BF16)
            w = w + jnp.where(r2_ref[h, :, ws] <= jq, p2_ref[h, :, ws], jnp.zeros((), BF16)) * c1
        for tc in range(PE_WT // LT):
            cs = slice(tw * PE_WT + tc * LT, tw * PE_WT + (tc + 1) * LT)
            xh = hid_ref[rs, cs]
            t = jnp.tanh(xh * (xh * xh * (0.044715 * _GELU_C) + _GELU_C))
            xw = xh * w[:, tc * LT:(tc + 1) * LT].astype(F32)
            gh_ref[rs, cs] = (xw * t + xw).astype(BF16)


def _peer_body(x_ref, u_ref, vt_ref, jq_ref, r2_ref, c1_ref, p2_ref, ng_ref, nb_ref, o_ref,
               acc, xb_ref, hid0, hid1, gh0, gh1):
    g = pl.program_id(1)
    EB = PE_EB

    @pl.when(g == 0)
    def _():
        acc[...] = jnp.zeros_like(acc)
        xb_ref[...] = x_ref[...].astype(BF16)
        hid1[...] = jnp.zeros_like(hid1)
        gh0[...] = jnp.zeros_like(gh0)

    routing = (jq_ref, r2_ref, c1_ref, p2_ref)
    chunks = [slice(tw * PE_WT, (tw + 1) * PE_WT) for tw in range(PE_TM // PE_WT)]
    for tw, ws in enumerate(chunks):
        _peer_gate_gelu(2 * g - 1, tw, hid1, gh1, *routing)
        hid0[:, ws] = _dot_nt(u_ref[0:EB, :], xb_ref[ws, :])
        acc[:, ws] += _dot(vt_ref[:, 0:EB], gh0[:, ws])
    for tw, ws in enumerate(chunks):
        _peer_gate_gelu(2 * g, tw, hid0, gh0, *routing)
        hid1[:, ws] = _dot_nt(u_ref[EB:2 * EB, :], xb_ref[ws, :])
        acc[:, ws] += _dot(vt_ref[:, EB:2 * EB], gh1[:, ws])

    @pl.when(g == pl.num_programs(1) - 1)
    def _():
        o_ref[...] = _layer_norm(ALPHA * x_ref[...] + acc[...].T, ng_ref[...], nb_ref[...])


def _peer(x1, u_b, vt_b, jq, r2, c1, p2, ng, nb):
    T, D = x1.shape
    TM, EB = PE_TM, PE_EB
    H = PEER_HEADS
    steps = PE_NB // 2 + 1
    rt = pl.BlockSpec((H, N_KEYS, TM), lambda i, g: (0, 0, i))
    full = lambda a: pl.BlockSpec(a.shape, lambda i, g: (0,) * a.ndim)
    return pl.pallas_call(
        _peer_body,
        grid=(T // TM, steps),
        in_specs=[pl.BlockSpec((TM, D), lambda i, g: (i, 0)),
                  pl.BlockSpec((2 * EB, D), lambda i, g: (jnp.minimum(g, steps - 2), 0)),
                  pl.BlockSpec((D, 2 * EB), lambda i, g: (0, jnp.maximum(g - 1, 0))),
                  rt, rt, rt, rt, full(ng), full(nb)],
        out_specs=pl.BlockSpec((TM, D), lambda i, g: (i, 0)),
        out_shape=jax.ShapeDtypeStruct((T, D), F32),
        scratch_shapes=[pltpu.VMEM((D, TM), F32), pltpu.VMEM((TM, D), BF16),
                        pltpu.VMEM((EB, TM), F32), pltpu.VMEM((EB, TM), F32),
                        pltpu.VMEM((EB, TM), BF16), pltpu.VMEM((EB, TM), BF16)],
        compiler_params=_params("arbitrary", "arbitrary"),
        name="peer_experts",
    )(x1, u_b, vt_b, jq, r2, c1, p2, ng, nb)


def _static_tables(S):
    nsb = S // SEL_BLOCK
    nc = S // CMP_STRIDE
    c0 = np.arange(nc) * CMP_STRIDE
    s0 = np.arange(nsb) * SEL_BLOCK
    ov = np.clip(np.minimum(c0[:, None] + CMP_BLOCK, s0[None, :] + SEL_BLOCK)
                 - np.maximum(c0[:, None], s0[None, :]), 0, None).astype(np.float32) / CMP_BLOCK
    ov[nc - 1:, :] = 0.0
    expand = np.zeros((SEL_PAD + S, 128), np.float32)
    expand[SEL_PAD + np.arange(S), np.arange(S) // SEL_BLOCK] = 1.0
    hid = np.arange(RWKV_DIM) // RWKV_HEAD_DIM
    bd = (hid[:, None] == hid[None, :]).astype(np.float32)
    t = np.arange(RW_TM)
    tri = ((t[:, None] >= t[None, :]) & (t[:, None] // RWKV_CHUNK == t[None, :] // RWKV_CHUNK)).astype(np.float32)
    return jnp.asarray(ov.T), jnp.asarray(expand, BF16), jnp.asarray(bd), jnp.asarray(tri)


def _nsa(qt, kv, gate, rel_bias, cmp_pos, cmp_w1, cmp_b1, cmp_w2, cmp_b2, B, S, tables):
    G = NSA_GROUPS
    ovt, expand = tables
    kv5 = kv.reshape(B, S, 6, G, HEAD_DIM)
    chunks = lambda a: a.reshape(B, S // CMP_STRIDE, CMP_STRIDE, G, HEAD_DIM).transpose(0, 3, 1, 2, 4) \
        .reshape(B, G, S // CMP_STRIDE, CMP_STRIDE * HEAD_DIM)
    xc = jnp.stack([chunks(kv5[:, :, 0]), chunks(kv5[:, :, 1])])
    kvc = _compress(xc, cmp_pos.reshape(2, 1, CMP_BLOCK * HEAD_DIM), cmp_w1, cmp_b1.reshape(2, 1, CMP_HIDDEN),
                    cmp_w2, cmp_b2.reshape(2, 1, HEAD_DIM))
    bias_cmp, dsel, dwin = _bias_tables(rel_bias, S)
    o_cmp, sel = _cmp_attn(qt, kvc, bias_cmp, ovt)
    heads_first = lambda a, pad: jnp.pad(a.transpose(0, 2, 1, 3).astype(BF16), ((0, 0), (0, 0), (pad, 0), (0, 0)))
    ks, vs = heads_first(kv5[:, :, 2], SEL_PAD), heads_first(kv5[:, :, 3], SEL_PAD)
    kw, vw = heads_first(kv5[:, :, 4], WINDOW), heads_first(kv5[:, :, 5], WINDOW)
    gate_g = gate.reshape(B, S, GATE_PAD)[:, :, :3 * NSA_HEADS].reshape(B, S, G, 3 * NSA_HPG).transpose(0, 2, 3, 1)
    gate_g = jnp.pad(gate_g, ((0, 0), (0, 0), (0, 16 - 3 * NSA_HPG), (0, 0)))
    return _nsa_main(qt, ks, vs, kw, vw, sel, expand, dsel, dwin, gate_g, o_cmp)


def kernel(x, ln_in_g, ln_in_b, rel_bias, w_in, token_mu, cmp_pos, cmp_w1, cmp_b1, cmp_w2, cmp_b2, rwkv_w0, rwkv_w2, rwkv_a0, rwkv_a2, rwkv_g2, rwkv_k_k, rwkv_k_a, rwkv_r_k, rwkv_lnx_g, rwkv_lnx_b, w_o_nsa, w_o_rwkv, w_out, ln_mix_g, ln_mix_b, peer_w_query, peer_sub_keys, peer_u, peer_v, ln_ffn_g, ln_ffn_b):
    B, S, D = x.shape
    T = B * S
    row = lambda a: a.reshape(1, -1)
    ovt, expand, bd, tri = _static_tables(S)

    segs = jnp.split(w_in[0], IN_SPLITS, axis=1)
    w_pad = jnp.concatenate(segs[1:7] + [jnp.pad(segs[7], ((0, 0), (0, GATE_PAD - 3 * NSA_HEADS)))] + segs[8:],
                            axis=1).astype(BF16)
    wq_t = (segs[0] * ATTN_SCALE).T.astype(BF16)
    h, qt, kv, gate, rw, mg = _inproj(x.reshape(T, D), row(ln_in_g), row(ln_in_b), wq_t, w_pad, B, S)

    y_nsa = _nsa(qt, kv, gate, rel_bias, cmp_pos[0], cmp_w1[0], cmp_b1[0], cmp_w2[0], cmp_b2[0], B, S, (ovt, expand))

    zpad = jnp.zeros((LORA_W, RWKV_DIM), F32)
    w2p = jnp.concatenate([rwkv_w2[0], zpad], axis=0)
    a2p = jnp.concatenate([zpad, rwkv_a2[0]], axis=0)
    ab, rb, bt, kt, v, pc, g, bonus = _rwkv_prep(
        rw.reshape(B, S, RWKV_COLS), row(token_mu[0]), row(rwkv_w0[0]), w2p, row(rwkv_a0[0]), a2p, rwkv_g2[0],
        row(rwkv_k_k[0]), row(rwkv_k_a[0]), row(rwkv_r_k[0]), bd, tri)
    y_hm = _rwkv_scan(ab, rb, bt, kt, v, pc.reshape(B, RWKV_HEADS, S // RWKV_CHUNK, 1, RWKV_HEAD_DIM))

    x1 = _mix(h.reshape(B, S, D), y_nsa, y_hm, bonus, g, mg.reshape(B, S, 2 * D), row(rwkv_lnx_g[0]),
              row(rwkv_lnx_b[0]), w_o_nsa[0].astype(BF16), w_o_rwkv[0].astype(BF16), w_out[0].astype(BF16),
              row(ln_mix_g[0]), row(ln_mix_b[0])).reshape(T, D)

    jq, r2, c1, p2 = _route(x1, peer_w_query[0].T, peer_sub_keys[0])
    out = _peer(x1, peer_u[0].astype(BF16), peer_v[0].T.astype(BF16), jq, r2, c1, p2,
                row(ln_ffn_g[0]), row(ln_ffn_b[0]))
    return out.reshape(B, S, D)
```
